```python
import math
import jax, jax.numpy as jnp
from jax import lax
import numpy as np

D_MODEL = 1024
BATCH = 8
SEQ = 4096
DEPTH = 4

GROUP_WIDTH = D_MODEL // 4
N_GROUPS = 5
MIX_WIDTH = N_GROUPS * GROUP_WIDTH
POOL_WINDOWS = (2, 4, 8, 16)
POOL_CH = GROUP_WIDTH // len(POOL_WINDOWS)
CONV_WIDTH = 31
SHORT_CONV_WIDTH = 3
DIFF_HEADS = 4
DIFF_HEAD_DIM = GROUP_WIDTH // (2 * DIFF_HEADS)
MEM_HEADS = 4
MEM_HEAD_DIM = GROUP_WIDTH // MEM_HEADS
N_MEM = 256
Q_BLOCK = 128
PEER_HEADS = 8
PEER_N_KEYS = 128
PEER_N_EXPERTS = PEER_N_KEYS * PEER_N_KEYS
PEER_TOPK = 16
PEER_QUERY_DIM = 256
PEER_HALF = PEER_QUERY_DIM // 2
PEER_CHUNK = 128
EPS = 1e-6
IN_COLS = 10 * GROUP_WIDTH
IN_SPLITS = (GROUP_WIDTH, 3 * GROUP_WIDTH, 6 * GROUP_WIDTH, 9 * GROUP_WIDTH)

kernel_name = 'hybrid_parallel_groups_peer_encoder'


def rmsnorm(x, g):
    xf = x.astype(jnp.float32)
    xf = xf * lax.rsqrt(jnp.mean(xf * xf, axis=-1, keepdims=True) + EPS)
    return xf.astype(x.dtype) * g


def layernorm(x, g, b):
    xf = x.astype(jnp.float32)
    mu = jnp.mean(xf, axis=-1, keepdims=True)
    var = jnp.mean(jnp.square(xf - mu), axis=-1, keepdims=True)
    return ((xf - mu) * lax.rsqrt(var + EPS)).astype(x.dtype) * g + b


def depthwise_conv(x, w):
    k = w.shape[0]
    pad = k // 2
    return lax.conv_general_dilated(x, w[:, None, :].astype(x.dtype), (1,), [(pad, pad)],
                                    dimension_numbers=('NWC', 'WIO', 'NWC'),
                                    feature_group_count=x.shape[-1])


def alibi_slopes(n):
    return 2.0 ** (-8.0 * jnp.arange(1, n + 1, dtype=jnp.float32) / n)


def pool_mixer(u, w, scale):
    b, s, _ = u.shape
    cs = jnp.concatenate([jnp.zeros((b, 1, GROUP_WIDTH), jnp.float32),
                          jnp.cumsum(u.astype(jnp.float32), axis=1)], axis=1)
    t = jnp.arange(s)
    outs = []
    for gi, win in enumerate(POOL_WINDOWS):
        lo = jnp.clip(t - win // 2, 0, s)
        hi = jnp.clip(t + win // 2, 0, s)
        csg = cs[:, :, gi * POOL_CH:(gi + 1) * POOL_CH]
        outs.append((csg[:, hi] - csg[:, lo]) / (hi - lo).astype(jnp.float32)[None, :, None])
    pooled = jnp.concatenate(outs, axis=-1).astype(u.dtype) - u
    y = jnp.einsum('bsgc,gcd->bsgd', pooled.reshape(b, s, len(POOL_WINDOWS), POOL_CH), w)
    return y.reshape(b, s, GROUP_WIDTH) * scale


def conformer_conv(u, dw, ln_g, ln_b, pw):
    a, gate = jnp.split(u, 2, axis=-1)
    h = a * jax.nn.sigmoid(gate)
    h = depthwise_conv(h, dw)
    h = jax.nn.silu(layernorm(h, ln_g, ln_b))
    return h @ pw


def diff_attention(u, qk_g, lam_p, subln_g, layer_idx):
    b, s, _ = u.shape
    q, k, v = jnp.split(u, 3, axis=-1)
    q = rmsnorm(q.reshape(b, s, 2, DIFF_HEADS, DIFF_HEAD_DIM), qk_g[0])
    k = rmsnorm(k.reshape(b, s, 2, DIFF_HEADS, DIFF_HEAD_DIM), qk_g[1])
    v = v.reshape(b, s, DIFF_HEADS, 2 * DIFF_HEAD_DIM)
    lam_init = 0.8 - 0.6 * math.exp(-0.3 * layer_idx)
    lp = lam_p.astype(jnp.float32)
    lam = jnp.exp(jnp.sum(lp[0] * lp[1])) - jnp.exp(jnp.sum(lp[2] * lp[3])) + lam_init
    slopes = alibi_slopes(DIFF_HEADS)
    scale = DIFF_HEAD_DIM ** -0.5
    nb = s // Q_BLOCK
    qb = q.reshape(b, nb, Q_BLOCK, 2, DIFF_HEADS, DIFF_HEAD_DIM).transpose(1, 0, 2, 3, 4, 5)
    starts = jnp.arange(nb, dtype=jnp.int32) * Q_BLOCK
    kpos = jnp.arange(s, dtype=jnp.int32)

    def block(args):
        qblk, start = args
        sc = jnp.einsum('bqjhd,bkjhd->bjhqk', qblk, k).astype(jnp.float32) * scale
        qpos = start + jnp.arange(Q_BLOCK, dtype=jnp.int32)
        dist = jnp.abs(qpos[:, None] - kpos[None, :]).astype(jnp.float32)
        sc = sc - slopes[:, None, None] * dist
        p = jax.nn.softmax(sc, axis=-1)
        a = p[:, 0] - lam * p[:, 1]
        return jnp.einsum('bhqk,bkhe->bqhe', a.astype(v.dtype), v)

    o = lax.map(block, (qb, starts))
    o = o.transpose(1, 0, 2, 3, 4).reshape(b, s, DIFF_HEADS, 2 * DIFF_HEAD_DIM)
    o = rmsnorm(o, subln_g) * (1.0 - lam_init)
    return o.reshape(b, s, GROUP_WIDTH)


def short_conv(u, w):
    h, bg, cg = jnp.split(u, 3, axis=-1)
    return bg * depthwise_conv(cg * h, w)


def memory_attention(u, mem, mem_g, w_kv, qk_g):
    b, s, _ = u.shape
    m = mem.shape[1]
    kv = rmsnorm(mem, mem_g) @ w_kv
    k, v = jnp.split(kv, 2, axis=-1)
    q = rmsnorm(u.reshape(b, s, MEM_HEADS, MEM_HEAD_DIM), qk_g[0])
    k = rmsnorm(k.reshape(b, m, MEM_HEADS, MEM_HEAD_DIM), qk_g[1])
    v = v.reshape(b, m, MEM_HEADS, MEM_HEAD_DIM)
    sc = jnp.einsum('bshd,bmhd->bhsm', q, k).astype(jnp.float32) * (MEM_HEAD_DIM ** -0.5)
    p = jax.nn.softmax(sc, axis=-1).astype(v.dtype)
    return jnp.einsum('bhsm,bmhe->bshe', p, v).reshape(b, s, GROUP_WIDTH)


def peer_ffn(xn, wq, subkeys, u_tab, v_tab):
    b, s, d = xn.shape
    q = (xn @ wq).reshape(b, s, PEER_HEADS, 2, PEER_HALF)
    sc = jnp.einsum('bshjc,hjkc->bshjk', q, subkeys).astype(jnp.float32)
    top_s, top_i = lax.top_k(sc, PEER_TOPK)
    cand = (top_s[..., 0, :, None] + top_s[..., 1, None, :]).reshape(b, s, PEER_HEADS, PEER_TOPK * PEER_TOPK)
    best, ci = lax.top_k(cand, PEER_TOPK)
    i1 = jnp.take_along_axis(top_i[..., 0, :], ci // PEER_TOPK, axis=-1)
    i2 = jnp.take_along_axis(top_i[..., 1, :], ci % PEER_TOPK, axis=-1)
    idx = i1 * PEER_N_KEYS + i2
    g = jax.nn.softmax(best, axis=-1)
    t = b * s
    nc = t // PEER_CHUNK
    idx_c = idx.reshape(nc, PEER_CHUNK, PEER_HEADS * PEER_TOPK)
    g_c = g.reshape(nc, PEER_CHUNK, PEER_HEADS * PEER_TOPK)
    x_c = xn.reshape(nc, PEER_CHUNK, d)

    def chunk(args):
        ic, gc, xc = args
        act = jax.nn.gelu(jnp.einsum('ckd,cd->ck', u_tab[ic], xc), approximate=False)
        return jnp.einsum('ck,ckd->cd', (gc * act).astype(v_tab.dtype), v_tab[ic])

    return lax.map(chunk, (idx_c, g_c, x_c)).reshape(b, s, d)


def setup_inputs(seed: int = 0) -> dict:
    key = jax.random.key(seed)
    ks = jax.random.split(key, 24)
    f32 = jnp.float32
    nrm = lambda k, shape, sc: jax.random.normal(k, shape, f32) * sc
    gain = lambda k, shape: 1.0 + 0.02 * jax.random.normal(k, shape, f32)
    L = DEPTH
    return {
        'x': nrm(ks[0], (BATCH, SEQ, D_MODEL), 1.0),
        'mem': nrm(ks[1], (BATCH, N_MEM, D_MODEL), 1.0),
        'norm_mix': gain(ks[2], (L, D_MODEL)),
        'w_in': nrm(ks[3], (L, D_MODEL, IN_COLS), D_MODEL ** -0.5),
        'pool_w': nrm(ks[4], (L, len(POOL_WINDOWS), POOL_CH, POOL_CH), POOL_CH ** -0.5),
        'pool_scale': 1.0 + 0.1 * jax.random.normal(ks[5], (L, GROUP_WIDTH), f32),
        'conv_dw': nrm(ks[6], (L, CONV_WIDTH, GROUP_WIDTH), CONV_WIDTH ** -0.5),
        'conv_ln_g': gain(ks[7], (L, GROUP_WIDTH)),
        'conv_ln_b': nrm(ks[8], (L, GROUP_WIDTH), 0.02),
        'conv_pw': nrm(ks[9], (L, GROUP_WIDTH, GROUP_WIDTH), GROUP_WIDTH ** -0.5),
        'diff_qk_norm': gain(ks[10], (L, 2, DIFF_HEAD_DIM)),
        'diff_lambda': nrm(ks[11], (L, 4, DIFF_HEAD_DIM), 0.1),
        'diff_subln': gain(ks[12], (L, 2 * DIFF_HEAD_DIM)),
        'sconv_w': nrm(ks[13], (L, SHORT_CONV_WIDTH, GROUP_WIDTH), SHORT_CONV_WIDTH ** -0.5),
        'mem_norm': gain(ks[14], (L, D_MODEL)),
        'w_mem_kv': nrm(ks[15], (L, D_MODEL, 2 * GROUP_WIDTH), D_MODEL ** -0.5),
        'mem_qk_norm': gain(ks[16], (L, 2, MEM_HEAD_DIM)),
        'group_norm': gain(ks[17], (L, MIX_WIDTH)),
        'w_out': nrm(ks[18], (L, MIX_WIDTH, D_MODEL), MIX_WIDTH ** -0.5),
        'norm_ffn': gain(ks[19], (L, D_MODEL)),
        'peer_wq': nrm(ks[20], (L, D_MODEL, PEER_HEADS * PEER_QUERY_DIM), D_MODEL ** -0.5),
        'peer_subkeys': nrm(ks[21], (L, PEER_HEADS, 2, PEER_N_KEYS, PEER_HALF), PEER_HALF ** -0.5),
        'peer_u': nrm(ks[22], (L, PEER_N_EXPERTS, D_MODEL), D_MODEL ** -0.5),
        'peer_v': nrm(ks[23], (L, PEER_N_EXPERTS, D_MODEL), 0.1),
    }


def reference(x, mem, norm_mix, w_in, pool_w, pool_scale, conv_dw, conv_ln_g, conv_ln_b, conv_pw,
              diff_qk_norm, diff_lambda, diff_subln, sconv_w, mem_norm, w_mem_kv, mem_qk_norm,
              group_norm, w_out, norm_ffn, peer_wq, peer_subkeys, peer_u, peer_v):
    b, s, _ = x.shape
    for l in range(DEPTH):
        xn = rmsnorm(x, norm_mix[l])
        proj = xn @ w_in[l]
        u_pool, u_conv, u_diff, u_sc, u_mem = jnp.split(proj, IN_SPLITS, axis=-1)
        y_pool = pool_mixer(u_pool, pool_w[l], pool_scale[l])
        y_conv = conformer_conv(u_conv, conv_dw[l], conv_ln_g[l], conv_ln_b[l], conv_pw[l])
        y_diff = diff_attention(u_diff, diff_qk_norm[l], diff_lambda[l], diff_subln[l], l)
        y_sc = short_conv(u_sc, sconv_w[l])
        y_mem = memory_attention(u_mem, mem, mem_norm[l], w_mem_kv[l], mem_qk_norm[l])
        cat = jnp.stack([y_pool, y_conv, y_diff, y_sc, y_mem], axis=2)
        cat = rmsnorm(cat, group_norm[l].reshape(N_GROUPS, GROUP_WIDTH)).reshape(b, s, MIX_WIDTH)
        x = x + cat @ w_out[l]
        x = x + peer_ffn(rmsnorm(x, norm_ffn[l]), peer_wq[l], peer_subkeys[l], peer_u[l], peer_v[l])
    return x
```

```python
import functools
import math

import jax
import jax.numpy as jnp
from jax import lax
from jax.experimental import pallas as pl
from jax.experimental.pallas import tpu as pltpu

F32 = jnp.float32
BF16 = jnp.bfloat16

D_MODEL = 1024
GROUP_WIDTH = 256
N_GROUPS = 5
POOL_HALF_WINDOWS = (1, 2, 4, 8)
POOL_CH = 64
CONV_WIDTH = 31
SHORT_CONV_WIDTH = 3
DIFF_HEADS = 4
DIFF_HEAD_DIM = 32
MEM_HEADS = 4
MEM_HEAD_DIM = 64
PEER_HEADS = 8
PEER_N_KEYS = 128
PEER_TOPK = 16
EPS = 1e-6

HALO = 16
VMEM_LIMIT = 56 * 1024 * 1024

TM_IN = 512
TQ_ATT = 256
TS_MIX = 512
TM_RET = 256
TM_EXP = 512
A_BLOCK = 8
NEG = -1e30


def _params(sem):
    return pltpu.CompilerParams(dimension_semantics=sem, vmem_limit_bytes=VMEM_LIMIT)


def _rms(x, g):
    return x * lax.rsqrt(jnp.mean(x * x, axis=-1, keepdims=True) + EPS) * g


def _seg_rms(x, g, seg_mean):
    ms = jnp.dot(x * x, seg_mean, precision=lax.Precision.HIGHEST, preferred_element_type=F32)
    return x * lax.rsqrt(ms + EPS) * g


def _lane_range_mask(shape, lo, hi):
    lane = lax.broadcasted_iota(jnp.int32, shape, len(shape) - 1)
    return (lane >= lo) & (lane < hi)


def _inproj_kernel(x_ref, g_ref, w_ref, gq_ref, gk_ref, seg_ref,
                   upool_ref, uconv_ref, usc_ref, umem_ref, q_ref, kt_ref, v_ref):
    xb = _rms(x_ref[...], g_ref[...]).astype(BF16)

    def proj(lo, hi):
        return jnp.dot(xb, w_ref[:, lo:hi], preferred_element_type=F32)

    gw = GROUP_WIDTH
    upool_ref[...] = proj(0, gw)
    uconv_ref[...] = proj(gw, 3 * gw)
    q = proj(3 * gw, 4 * gw)
    k = proj(4 * gw, 5 * gw)
    v_ref[...] = proj(5 * gw, 6 * gw).astype(BF16)
    usc_ref[...] = proj(6 * gw, 9 * gw)
    umem_ref[...] = proj(9 * gw, 10 * gw)
    seg = seg_ref[...]
    q_ref[...] = (_seg_rms(q, gq_ref[...], seg) * (DIFF_HEAD_DIM ** -0.5)).astype(BF16)
    kt_ref[0] = _seg_rms(k, gk_ref[...], seg).T.astype(BF16)


def _inproj(x2d, g, w, gq, gk, seg32, batch, seq):
    t = x2d.shape[0]
    tm = min(TM_IN, seq)
    nst = seq // tm
    gw = GROUP_WIDTH
    row = lambda c: pl.BlockSpec((tm, c), lambda i: (i, 0))
    full = lambda a: pl.BlockSpec(a.shape, lambda i: (0,) * a.ndim)
    return pl.pallas_call(
        _inproj_kernel,
        grid=(t // tm,),
        in_specs=[row(D_MODEL), full(g), full(w), full(gq), full(gk), full(seg32)],
        out_specs=[row(gw), row(2 * gw), row(3 * gw), row(gw), row(gw),
                   pl.BlockSpec((1, gw, tm), lambda i: (i // nst, 0, i % nst)), row(gw)],
        out_shape=[jax.ShapeDtypeStruct((t, gw), F32), jax.ShapeDtypeStruct((t, 2 * gw), F32),
                   jax.ShapeDtypeStruct((t, 3 * gw), F32), jax.ShapeDtypeStruct((t, gw), F32),
                   jax.ShapeDtypeStruct((t, gw), BF16), jax.ShapeDtypeStruct((batch, gw, seq), BF16),
                   jax.ShapeDtypeStruct((t, gw), BF16)],
        compiler_params=_params(("parallel",)),
        name="in_projection",
    )(x2d, g, w, gq, gk, seg32)


def _memkv_kernel(mem_ref, g_ref, w_ref, gk_ref, seg_ref, kt_ref, v_ref):
    mn = _rms(mem_ref[0], g_ref[...]).astype(BF16)
    kv = jnp.dot(mn, w_ref[...], preferred_element_type=F32)
    k = _seg_rms(kv[:, :GROUP_WIDTH], gk_ref[...], seg_ref[...])
    kt_ref[0] = k.T.astype(BF16)
    v_ref[0] = kv[:, GROUP_WIDTH:].astype(BF16)


def _memkv(mem, g, w, gk, seg64):
    b, m, _ = mem.shape
    gw = GROUP_WIDTH
    full = lambda a: pl.BlockSpec(a.shape, lambda i: (0,) * a.ndim)
    return pl.pallas_call(
        _memkv_kernel,
        grid=(b,),
        in_specs=[pl.BlockSpec((1, m, D_MODEL), lambda i: (i, 0, 0)), full(g), full(w), full(gk), full(seg64)],
        out_specs=[pl.BlockSpec((1, gw, m), lambda i: (i, 0, 0)), pl.BlockSpec((1, m, gw), lambda i: (i, 0, 0))],
        out_shape=[jax.ShapeDtypeStruct((b, gw, m), BF16), jax.ShapeDtypeStruct((b, m, gw), BF16)],
        compiler_params=_params(("parallel",)),
        name="memory_kv",
    )(mem, g, w, gk, seg64)


def _diffattn_kernel(q_ref, kt_ref, v_ref, lam_ref, laminit_ref, gsub_ref, seg_ref, o_ref):
    tq = q_ref.shape[0]
    seq = kt_ref.shape[2]
    lp = lam_ref[...]
    lam_init = laminit_ref[...]
    lam = (jnp.exp(jnp.sum(lp[0:1] * lp[1:2], axis=-1, keepdims=True))
           - jnp.exp(jnp.sum(lp[2:3] * lp[3:4], axis=-1, keepdims=True)) + lam_init)
    qpos = pl.program_id(1) * tq + lax.broadcasted_iota(jnp.int32, (tq, seq), 0)
    kpos = lax.broadcasted_iota(jnp.int32, (tq, seq), 1)
    dist = jnp.abs(qpos - kpos).astype(F32)
    q = q_ref[...]
    kt = kt_ref[0]
    v = v_ref[...]
    o = jnp.zeros((tq, GROUP_WIDTH), F32)
    half = DIFF_HEADS * DIFF_HEAD_DIM
    for h in range(DIFF_HEADS):
        slope = 2.0 ** (-8.0 * (h + 1) / DIFF_HEADS)
        bias = slope * dist
        probs = []
        for j in range(2):
            lo = j * half + h * DIFF_HEAD_DIM
            qm = jnp.where(_lane_range_mask(q.shape, lo, lo + DIFF_HEAD_DIM), q, jnp.zeros_like(q))
            s = jnp.dot(qm, kt, preferred_element_type=F32) - bias
            e = jnp.exp(s - jnp.max(s, axis=-1, keepdims=True))
            probs.append(e * (1.0 / jnp.sum(e, axis=-1, keepdims=True)))
        a = (probs[0] - lam * probs[1]).astype(BF16)
        oh = jnp.dot(a, v, preferred_element_type=F32)
        vlo = h * 2 * DIFF_HEAD_DIM
        o = jnp.where(_lane_range_mask(o.shape, vlo, vlo + 2 * DIFF_HEAD_DIM), oh, o)
    o_ref[...] = _seg_rms(o, gsub_ref[...], seg_ref[...]) * (1.0 - lam_init)


def _diffattn(q, kt, v, lam_p, lam_init, gsub, seg64, batch, seq):
    t = q.shape[0]
    gw = GROUP_WIDTH
    tq = min(TQ_ATT, seq)
    nq = seq // tq
    full = lambda a: pl.BlockSpec(a.shape, lambda b, i: (0,) * a.ndim)
    return pl.pallas_call(
        _diffattn_kernel,
        grid=(batch, nq),
        in_specs=[pl.BlockSpec((tq, gw), lambda b, i: (b * nq + i, 0)),
                  pl.BlockSpec((1, gw, seq), lambda b, i: (b, 0, 0)),
                  pl.BlockSpec((seq, gw), lambda b, i: (b, 0)),
                  full(lam_p), full(lam_init), full(gsub), full(seg64)],
        out_specs=pl.BlockSpec((tq, gw), lambda b, i: (b * nq + i, 0)),
        out_shape=jax.ShapeDtypeStruct((t, gw), F32),
        compiler_params=_params(("parallel", "parallel")),
        name="diff_attention",
    )(q, kt, v, lam_p, lam_init, gsub, seg64)


def _mixer_kernel(x_ref, up_ref, up_prev, up_next, uc_ref, uc_prev, uc_next, us_ref, us_prev, us_next,
                  um_ref, yd_ref, kmt_ref, vm_ref,
                  poolw_ref, pools_ref, dw_ref, lng_ref, lnb_ref, pw_ref, sw_ref, gqm_ref, seg_ref,
                  gn_ref, wout_ref, o_ref, ext_pool, ext_conv, ext_sc, *, seq):
    ts = x_ref.shape[0]
    gw = GROUP_WIDTH
    nst = seq // ts
    it = pl.program_id(0) % nst
    keep_prev = jnp.where(it > 0, 1.0, 0.0)
    keep_next = jnp.where(it < nst - 1, 1.0, 0.0)

    def fill(ext, prev, main, nxt):
        ext[0:HALO, :] = prev * keep_prev
        ext[HALO:HALO + ts, :] = main
        ext[HALO + ts:HALO + ts + HALO, :] = nxt * keep_next

    def shifted(ext, k):
        return ext[HALO + k:HALO + k + ts, :]

    fill(ext_pool, up_prev[...], up_ref[...], up_next[...])
    u = up_ref[...]
    lane = lax.broadcasted_iota(jnp.int32, (ts, gw), 1)
    pos = it * ts + lax.broadcasted_iota(jnp.int32, (ts, gw), 0)
    wsum = jnp.zeros((ts, gw), F32)
    halfw = jnp.zeros((ts, gw), jnp.int32)
    run = None
    prev_half = 0
    for gi, hw in enumerate(POOL_HALF_WINDOWS):
        for k in list(range(-hw, -prev_half)) + list(range(prev_half, hw)):
            term = shifted(ext_pool, k)
            run = term if run is None else run + term
        prev_half = hw
        in_group = (lane >= gi * POOL_CH) & (lane < (gi + 1) * POOL_CH)
        wsum = jnp.where(in_group, run, wsum)
        halfw = jnp.where(in_group, hw, halfw)
    cnt = (jnp.minimum(pos + halfw, seq) - jnp.maximum(pos - halfw, 0)).astype(F32)
    pooled = wsum / cnt - u
    y_pool = jnp.dot(pooled.astype(BF16), poolw_ref[...], preferred_element_type=F32) * pools_ref[...]

    def glu(ucv):
        return ucv[:, :gw] * jax.nn.sigmoid(ucv[:, gw:])

    fill(ext_conv, glu(uc_prev[...]), glu(uc_ref[...]), glu(uc_next[...]))
    pad = CONV_WIDTH // 2
    conv = jnp.zeros((ts, gw), F32)
    for k in range(CONV_WIDTH):
        conv = conv + shifted(ext_conv, k - pad) * dw_ref[k:k + 1, :]
    mu = jnp.mean(conv, axis=-1, keepdims=True)
    cen = conv - mu
    var = jnp.mean(cen * cen, axis=-1, keepdims=True)
    hln = cen * lax.rsqrt(var + EPS) * lng_ref[...] + lnb_ref[...]
    hact = hln * jax.nn.sigmoid(hln)
    y_conv = jnp.dot(hact.astype(BF16), pw_ref[...], preferred_element_type=F32)

    def ch(usv):
        return usv[:, 2 * gw:] * usv[:, :gw]

    fill(ext_sc, ch(us_prev[...]), ch(us_ref[...]), ch(us_next[...]))
    spad = SHORT_CONV_WIDTH // 2
    sconv = jnp.zeros((ts, gw), F32)
    for k in range(SHORT_CONV_WIDTH):
        sconv = sconv + shifted(ext_sc, k - spad) * sw_ref[k:k + 1, :]
    y_sc = us_ref[:, gw:2 * gw] * sconv

    qn = (_seg_rms(um_ref[...], gqm_ref[...], seg_ref[...]) * (MEM_HEAD_DIM ** -0.5)).astype(BF16)
    kmt = kmt_ref[0]
    vm = vm_ref[0]
    y_mem = jnp.zeros((ts, gw), F32)
    for h in range(MEM_HEADS):
        hmask = _lane_range_mask((ts, gw), h * MEM_HEAD_DIM, (h + 1) * MEM_HEAD_DIM)
        qm = jnp.where(hmask, qn, jnp.zeros_like(qn))
        s = jnp.dot(qm, kmt, preferred_element_type=F32)
        e = jnp.exp(s - jnp.max(s, axis=-1, keepdims=True))
        p = (e * (1.0 / jnp.sum(e, axis=-1, keepdims=True))).astype(BF16)
        y_mem = jnp.where(hmask, jnp.dot(p, vm, preferred_element_type=F32), y_mem)

    acc = x_ref[...]
    for gi, y in enumerate((y_pool, y_conv, yd_ref[...], y_sc, y_mem)):
        yn = _rms(y, gn_ref[gi:gi + 1, :]).astype(BF16)
        acc = acc + jnp.dot(yn, wout_ref[gi * gw:(gi + 1) * gw, :], preferred_element_type=F32)
    o_ref[...] = acc


def _mixers(x2d, upool, uconv, usc, umem, ydiff, kmt, vm, poolw, pools, dw, lng, lnb, pw, sw, gqm, seg64,
            gn, wout, batch, seq):
    t = x2d.shape[0]
    gw = GROUP_WIDTH
    ts = min(TS_MIX, seq)
    nst = seq // ts
    r = ts // HALO
    last = t // HALO - 1
    row = lambda c: pl.BlockSpec((ts, c), lambda i: (i, 0))
    prev = lambda c: pl.BlockSpec((HALO, c), lambda i: (jnp.maximum(i * r - 1, 0), 0))
    nxt = lambda c: pl.BlockSpec((HALO, c), lambda i: (jnp.minimum((i + 1) * r, last), 0))
    full = lambda a: pl.BlockSpec(a.shape, lambda i: (0,) * a.ndim)
    perb = lambda a: pl.BlockSpec((1,) + a.shape[1:], lambda i: (i // nst, 0, 0))
    consts = (poolw, pools, dw, lng, lnb, pw, sw, gqm, seg64, gn, wout)
    return pl.pallas_call(
        functools.partial(_mixer_kernel, seq=seq),
        grid=(t // ts,),
        in_specs=[row(D_MODEL),
                  row(gw), prev(gw), nxt(gw),
                  row(2 * gw), prev(2 * gw), nxt(2 * gw),
                  row(3 * gw), prev(3 * gw), nxt(3 * gw),
                  row(gw), row(gw), perb(kmt), perb(vm)] + [full(a) for a in consts],
        out_specs=row(D_MODEL),
        out_shape=jax.ShapeDtypeStruct((t, D_MODEL), F32),
        scratch_shapes=[pltpu.VMEM((ts + 2 * HALO, gw), F32)] * 3,
        compiler_params=_params(("parallel",)),
        name="mixers_out_projection",
    )(x2d, upool, upool, upool, uconv, uconv, uconv, usc, usc, usc, umem, ydiff, kmt, vm, *consts)


def _top_values(s, n):
    vals = []
    for _ in range(n):
        m = jnp.max(s, axis=0, keepdims=True)
        vals.append(m)
        s = jnp.where(s == m, NEG, s)
    return vals


def _retrieval_kernel(x_ref, g_ref, wqt_ref, sk_ref, xnt_ref, thr_ref, e1_ref, s2_ref, e2_ref):
    tm = x_ref.shape[0]
    nk = PEER_N_KEYS
    n = PEER_TOPK + 1
    xb = _rms(x_ref[...], g_ref[...]).T.astype(BF16)
    xnt_ref[...] = xb
    qt = jnp.dot(wqt_ref[...], xb, preferred_element_type=F32)
    rows = 8 * ((n + 7) // 8)
    row_id = lax.broadcasted_iota(jnp.int32, (rows, tm), 0)
    for h in range(PEER_HEADS):
        sc = []
        for j in range(2):
            r0 = (2 * h + j) * nk
            sc.append(jnp.dot(sk_ref[2 * h + j], qt[r0:r0 + nk].astype(BF16), preferred_element_type=F32))
        v1 = _top_values(sc[0], n)
        v2 = _top_values(sc[1], n)
        v2_block = jnp.full((rows, tm), NEG, F32)
        for i in range(n):
            v2_block = jnp.where(row_id == i, v2[i], v2_block)
        cand = jnp.concatenate([v1[i] + v2_block for i in range(n)], axis=0)
        best = _top_values(cand, n)
        z = jnp.zeros((1, tm), F32)
        for i in range(PEER_TOPK):
            z = z + jnp.exp(best[i] - best[0])
        tau = 0.5 * (best[PEER_TOPK - 1] + best[PEER_TOPK])
        thr_ref[h] = tau - sc[0]
        e1_ref[h] = jnp.exp(sc[0] - v1[0])
        s2_ref[h] = sc[1]
        e2_ref[h] = jnp.exp(sc[1] - v2[0]) * (1.0 / z)


def _retrieval(x2d, g, wqt, sk):
    t = x2d.shape[0]
    tm = min(TM_RET, t)
    nk = PEER_N_KEYS
    full = lambda a: pl.BlockSpec(a.shape, lambda i: (0,) * a.ndim)
    tab = pl.BlockSpec((PEER_HEADS, nk, tm), lambda i: (0, 0, i))
    tab_shape = jax.ShapeDtypeStruct((PEER_HEADS, nk, t), F32)
    return pl.pallas_call(
        _retrieval_kernel,
        grid=(t // tm,),
        in_specs=[pl.BlockSpec((tm, D_MODEL), lambda i: (i, 0)), full(g), full(wqt), full(sk)],
        out_specs=[pl.BlockSpec((D_MODEL, tm), lambda i: (0, i)), tab, tab, tab, tab],
        out_shape=[jax.ShapeDtypeStruct((D_MODEL, t), BF16), tab_shape, tab_shape, tab_shape, tab_shape],
        compiler_params=_params(("parallel",)),
        name="peer_retrieval",
    )(x2d, g, wqt, sk)


def _expert_kernel(x_ref, xnt_ref, u_ref, vt_ref, thr_ref, e1_ref, s2_ref, e2_ref, o_ref, acc_ref):
    nk = PEER_N_KEYS
    eb = pl.program_id(1)

    @pl.when(eb == 0)
    def _():
        acc_ref[...] = jnp.zeros_like(acc_ref)

    ht = jnp.dot(u_ref[...], xnt_ref[...], preferred_element_type=F32)
    ws = []
    for al in range(A_BLOCK):
        hs = ht[al * nk:(al + 1) * nk]
        gate = jnp.zeros_like(hs)
        for h in range(PEER_HEADS):
            sel = jnp.where(s2_ref[h] >= thr_ref[h, al:al + 1, :], e2_ref[h], 0.0)
            gate = gate + e1_ref[h, al:al + 1, :] * sel
        act = 0.5 * hs * (1.0 + lax.erf(hs * math.sqrt(0.5)))
        ws.append((gate * act).astype(BF16))
    w = jnp.concatenate(ws, axis=0)
    acc_ref[...] += jnp.dot(vt_ref[...], w, preferred_element_type=F32)

    @pl.when(eb == pl.num_programs(1) - 1)
    def _():
        o_ref[...] = x_ref[...] + acc_ref[...].T


def _experts(x2d, xnt, u, vt, thr, e1, s2, e2):
    t = x2d.shape[0]
    tm = min(TM_EXP, t)
    nk = PEER_N_KEYS
    te = A_BLOCK * nk
    tab_a = pl.BlockSpec((PEER_HEADS, A_BLOCK, tm), lambda i, e: (0, e, i))
    tab_b = pl.BlockSpec((PEER_HEADS, nk, tm), lambda i, e: (0, 0, i))
    return pl.pallas_call(
        _expert_kernel,
        grid=(t // tm, nk // A_BLOCK),
        in_specs=[pl.BlockSpec((tm, D_MODEL), lambda i, e: (i, 0)),
                  pl.BlockSpec((D_MODEL, tm), lambda i, e: (0, i)),
                  pl.BlockSpec((te, D_MODEL), lambda i, e: (e, 0)),
                  pl.BlockSpec((D_MODEL, te), lambda i, e: (0, e)),
                  tab_a, tab_a, tab_b, tab_b],
        out_specs=pl.BlockSpec((tm, D_MODEL), lambda i, e: (i, 0)),
        out_shape=jax.ShapeDtypeStruct((t, D_MODEL), F32),
        scratch_shapes=[pltpu.VMEM((D_MODEL, tm), F32)],
        compiler_params=_params(("parallel", "arbitrary")),
        name="peer_experts",
    )(x2d, xnt, u, vt, thr, e1, s2, e2)


def _segment_mean_matrix(width, seg):
    idx = jnp.arange(width) // seg
    return (idx[:, None] == idx[None, :]).astype(F32) / seg


def _block_diag(w):
    g, c, _ = w.shape
    eye = jnp.eye(g, dtype=w.dtype)
    return (eye[:, None, :, None] * w[:, :, None, :]).reshape(g * c, g * c)


def kernel(x, mem, norm_mix, w_in, pool_w, pool_scale, conv_dw, conv_ln_g, conv_ln_b, conv_pw, diff_qk_norm,
           diff_lambda, diff_subln, sconv_w, mem_norm, w_mem_kv, mem_qk_norm, group_norm, w_out, norm_ffn,
           peer_wq, peer_subkeys, peer_u, peer_v):
    b, s, d = x.shape
    depth = w_in.shape[0]
    gw = GROUP_WIDTH
    seg32 = _segment_mean_matrix(gw, DIFF_HEAD_DIM)
    seg64 = _segment_mean_matrix(gw, MEM_HEAD_DIM)
    row = lambda a: a.reshape(1, -1)
    tile = lambda a, n: jnp.tile(a, n).reshape(1, -1)
    x2d = x.reshape(b * s, d)
    for l in range(depth):
        lam_init = jnp.full((1, 1), 0.8 - 0.6 * math.exp(-0.3 * l), F32)
        upool, uconv, usc, umem, q, kt, v = _inproj(
            x2d, row(norm_mix[l]), w_in[l].astype(BF16), tile(diff_qk_norm[l, 0], 2 * DIFF_HEADS),
            tile(diff_qk_norm[l, 1], 2 * DIFF_HEADS), seg32, b, s)
        kmt, vm = _memkv(mem, row(mem_norm[l]), w_mem_kv[l].astype(BF16), tile(mem_qk_norm[l, 1], MEM_HEADS), seg64)
        ydiff = _diffattn(q, kt, v, diff_lambda[l], lam_init, tile(diff_subln[l], DIFF_HEADS), seg64, b, s)
        x2d = _mixers(x2d, upool, uconv, usc, umem, ydiff, kmt, vm,
                      _block_diag(pool_w[l]).astype(BF16), row(pool_scale[l]), conv_dw[l], row(conv_ln_g[l]),
                      row(conv_ln_b[l]), conv_pw[l].astype(BF16), sconv_w[l], tile(mem_qk_norm[l, 0], MEM_HEADS),
                      seg64, group_norm[l].reshape(N_GROUPS, gw), w_out[l].astype(BF16), b, s)
        xnt, thr, e1, s2, e2 = _retrieval(
            x2d, row(norm_ffn[l]), peer_wq[l].T.astype(BF16),
            peer_subkeys[l].reshape(2 * PEER_HEADS, PEER_N_KEYS, -1).astype(BF16))
        x2d = _experts(x2d, xnt, peer_u[l].astype(BF16), peer_v[l].T.astype(BF16), thr, e1, s2, e2)
    return x2d.reshape(b, s, d)
```

```python
import functools
import math

import jax
import jax.numpy as jnp
from jax import lax
from jax.experimental import pallas as pl
from jax.experimental.pallas import tpu as pltpu

F32 = jnp.float32
BF16 = jnp.bfloat16

D_MODEL = 1024
GROUP_WIDTH = 256
N_GROUPS = 5
POOL_HALF_WINDOWS = (1, 2, 4, 8)
POOL_CH = 64
CONV_WIDTH = 31
SHORT_CONV_WIDTH = 3
DIFF_HEADS = 4
DIFF_HEAD_DIM = 32
MEM_HEADS = 4
MEM_HEAD_DIM = 64
PEER_HEADS = 8
PEER_N_KEYS = 128
PEER_TOPK = 16
EPS = 1e-6

LANES = 128
HALO = 16
VMEM_LIMIT = 56 * 1024 * 1024

TM_IN = 512
TQ_ATT = 256
TS_MIX = 512
TM_RET = 256
TM_EXP = 512
A_BLOCK = 8
NEG = -1e30


def _params(sem):
    return pltpu.CompilerParams(dimension_semantics=sem, vmem_limit_bytes=VMEM_LIMIT)


def _rms(x, g):
    return x * lax.rsqrt(jnp.mean(x * x, axis=-1, keepdims=True) + EPS) * g


def _seg_rms(x, g, seg_mean):
    ms = jnp.dot(x * x, seg_mean, precision=lax.Precision.HIGHEST, preferred_element_type=F32)
    return x * lax.rsqrt(ms + EPS) * g


def _lane_range_mask(shape, lo, hi):
    lane = lax.broadcasted_iota(jnp.int32, shape, len(shape) - 1)
    return (lane >= lo) & (lane < hi)


def _inproj_kernel(x_ref, g_ref, w_ref, gq_ref, gk_ref, seg_ref,
                   upool_ref, uconv_ref, usc_ref, umem_ref, q_ref, kt_ref, v_ref):
    xb = _rms(x_ref[...], g_ref[...]).astype(BF16)

    def proj(lo, hi):
        return jnp.dot(xb, w_ref[:, lo:hi], preferred_element_type=F32)

    gw = GROUP_WIDTH
    upool_ref[...] = proj(0, gw)
    uconv_ref[...] = proj(gw, 3 * gw)
    q = proj(3 * gw, 4 * gw)
    k = proj(4 * gw, 5 * gw)
    v_ref[...] = proj(5 * gw, 6 * gw).astype(BF16)
    usc_ref[...] = proj(6 * gw, 9 * gw)
    umem_ref[...] = proj(9 * gw, 10 * gw)
    seg = seg_ref[...]
    q_ref[...] = (_seg_rms(q, gq_ref[...], seg) * (DIFF_HEAD_DIM ** -0.5)).astype(BF16)
    kt_ref[0] = _seg_rms(k, gk_ref[...], seg).T.astype(BF16)


def _inproj(x2d, g, w, gq, gk, seg32, batch, seq):
    t = x2d.shape[0]
    tm = min(TM_IN, seq)
    nst = seq // tm
    gw = GROUP_WIDTH
    row = lambda c: pl.BlockSpec((tm, c), lambda i: (i, 0))
    full = lambda a: pl.BlockSpec(a.shape, lambda i: (0,) * a.ndim)
    return pl.pallas_call(
        _inproj_kernel,
        grid=(t // tm,),
        in_specs=[row(D_MODEL), full(g), full(w), full(gq), full(gk), full(seg32)],
        out_specs=[row(gw), row(2 * gw), row(3 * gw), row(gw), row(gw),
                   pl.BlockSpec((1, gw, tm), lambda i: (i // nst, 0, i % nst)), row(gw)],
        out_shape=[jax.ShapeDtypeStruct((t, gw), F32), jax.ShapeDtypeStruct((t, 2 * gw), F32),
                   jax.ShapeDtypeStruct((t, 3 * gw), F32), jax.ShapeDtypeStruct((t, gw), F32),
                   jax.ShapeDtypeStruct((t, gw), BF16), jax.ShapeDtypeStruct((batch, gw, seq), BF16),
                   jax.ShapeDtypeStruct((t, gw), BF16)],
        compiler_params=_params(("parallel",)),
        name="in_projection",
    )(x2d, g, w, gq, gk, seg32)


def _memkv_kernel(mem_ref, g_ref, w_ref, gk_ref, seg_ref, kt_ref, v_ref):
    mn = _rms(mem_ref[0], g_ref[...]).astype(BF16)
    kv = jnp.dot(mn, w_ref[...], preferred_element_type=F32)
    k = _seg_rms(kv[:, :GROUP_WIDTH], gk_ref[...], seg_ref[...])
    kt_ref[0] = k.T.astype(BF16)
    v_ref[0] = kv[:, GROUP_WIDTH:].astype(BF16)


def _memkv(mem, g, w, gk, seg64):
    b, m, _ = mem.shape
    gw = GROUP_WIDTH
    full = lambda a: pl.BlockSpec(a.shape, lambda i: (0,) * a.ndim)
    return pl.pallas_call(
        _memkv_kernel,
        grid=(b,),
        in_specs=[pl.BlockSpec((1, m, D_MODEL), lambda i: (i, 0, 0)), full(g), full(w), full(gk), full(seg64)],
        out_specs=[pl.BlockSpec((1, gw, m), lambda i: (i, 0, 0)), pl.BlockSpec((1, m, gw), lambda i: (i, 0, 0))],
        out_shape=[jax.ShapeDtypeStruct((b, gw, m), BF16), jax.ShapeDtypeStruct((b, m, gw), BF16)],
        compiler_params=_params(("parallel",)),
        name="memory_kv",
    )(mem, g, w, gk, seg64)


def _diffattn_kernel(q_ref, kt_ref, v_ref, lam_ref, laminit_ref, gsub_ref, seg_ref, o_ref):
    tq = q_ref.shape[0]
    seq = kt_ref.shape[2]
    lp = lam_ref[...]
    lam_init = laminit_ref[...]
    lam = (jnp.exp(jnp.sum(lp[0:1] * lp[1:2], axis=-1, keepdims=True))
           - jnp.exp(jnp.sum(lp[2:3] * lp[3:4], axis=-1, keepdims=True)) + lam_init)
    qpos = pl.program_id(1) * tq + lax.broadcasted_iota(jnp.int32, (tq, seq), 0)
    kpos = lax.broadcasted_iota(jnp.int32, (tq, seq), 1)
    dist = jnp.abs(qpos - kpos).astype(F32)
    q = q_ref[...]
    kt = kt_ref[0]
    v = v_ref[...]
    o = jnp.zeros((tq, GROUP_WIDTH), F32)
    half = DIFF_HEADS * DIFF_HEAD_DIM
    for h in range(DIFF_HEADS):
        slope = 2.0 ** (-8.0 * (h + 1) / DIFF_HEADS)
        bias = slope * dist
        probs = []
        for j in range(2):
            lo = j * half + h * DIFF_HEAD_DIM
            qm = jnp.where(_lane_range_mask(q.shape, lo, lo + DIFF_HEAD_DIM), q, jnp.zeros_like(q))
            s = jnp.dot(qm, kt, preferred_element_type=F32) - bias
            e = jnp.exp(s - jnp.max(s, axis=-1, keepdims=True))
            probs.append(e * (1.0 / jnp.sum(e, axis=-1, keepdims=True)))
        a = (probs[0] - lam * probs[1]).astype(BF16)
        oh = jnp.dot(a, v, preferred_element_type=F32)
        vlo = h * 2 * DIFF_HEAD_DIM
        o = jnp.where(_lane_range_mask(o.shape, vlo, vlo + 2 * DIFF_HEAD_DIM), oh, o)
    o_ref[...] = _seg_rms(o, gsub_ref[...], seg_ref[...]) * (1.0 - lam_init)


def _diffattn(q, kt, v, lam_p, lam_init, gsub, seg64, batch, seq):
    t = q.shape[0]
    gw = GROUP_WIDTH
    tq = min(TQ_ATT, seq)
    nq = seq // tq
    full = lambda a: pl.BlockSpec(a.shape, lambda b, i: (0,) * a.ndim)
    return pl.pallas_call(
        _diffattn_kernel,
        grid=(batch, nq),
        in_specs=[pl.BlockSpec((tq, gw), lambda b, i: (b * nq + i, 0)),
                  pl.BlockSpec((1, gw, seq), lambda b, i: (b, 0, 0)),
                  pl.BlockSpec((seq, gw), lambda b, i: (b, 0)),
                  full(lam_p), full(lam_init), full(gsub), full(seg64)],
        out_specs=pl.BlockSpec((tq, gw), lambda b, i: (b * nq + i, 0)),
        out_shape=jax.ShapeDtypeStruct((t, gw), F32),
        compiler_params=_params(("parallel", "parallel")),
        name="diff_attention",
    )(q, kt, v, lam_p, lam_init, gsub, seg64)


def _mixer_kernel(x_ref, up_ref, up_prev, up_next, uc_ref, uc_prev, uc_next, us_ref, us_prev, us_next,
                  um_ref, yd_ref, kmt_ref, vm_ref,
                  poolw_ref, pools_ref, dw_ref, lng_ref, lnb_ref, pw_ref, sw_ref, gqm_ref, seg_ref,
                  gn_ref, wout_ref, o_ref, ext_pool, ext_conv, ext_sc, *, seq):
    ts = x_ref.shape[0]
    gw = GROUP_WIDTH
    nst = seq // ts
    it = pl.program_id(0) % nst
    keep_prev = jnp.where(it > 0, 1.0, 0.0)
    keep_next = jnp.where(it < nst - 1, 1.0, 0.0)

    def fill(ext, prev, main, nxt):
        ext[0:HALO, :] = prev * keep_prev
        ext[HALO:HALO + ts, :] = main
        ext[HALO + ts:HALO + ts + HALO, :] = nxt * keep_next

    def shifted(ext, k):
        return ext[HALO + k:HALO + k + ts, :]

    fill(ext_pool, up_prev[...], up_ref[...], up_next[...])
    u = up_ref[...]
    lane = lax.broadcasted_iota(jnp.int32, (ts, gw), 1)
    pos = it * ts + lax.broadcasted_iota(jnp.int32, (ts, gw), 0)
    wsum = jnp.zeros((ts, gw), F32)
    halfw = jnp.zeros((ts, gw), jnp.int32)
    run = None
    prev_half = 0
    for gi, hw in enumerate(POOL_HALF_WINDOWS):
        for k in list(range(-hw, -prev_half)) + list(range(prev_half, hw)):
            term = shifted(ext_pool, k)
            run = term if run is None else run + term
        prev_half = hw
        in_group = (lane >= gi * POOL_CH) & (lane < (gi + 1) * POOL_CH)
        wsum = jnp.where(in_group, run, wsum)
        halfw = jnp.where(in_group, hw, halfw)
    cnt = (jnp.minimum(pos + halfw, seq) - jnp.maximum(pos - halfw, 0)).astype(F32)
    pooled = wsum / cnt - u
    y_pool = jnp.dot(pooled.astype(BF16), poolw_ref[...], preferred_element_type=F32) * pools_ref[...]

    def glu(ucv):
        return ucv[:, :gw] * jax.nn.sigmoid(ucv[:, gw:])

    fill(ext_conv, glu(uc_prev[...]), glu(uc_ref[...]), glu(uc_next[...]))
    pad = CONV_WIDTH // 2
    conv = jnp.zeros((ts, gw), F32)
    for k in range(CONV_WIDTH):
        conv = conv + shifted(ext_conv, k - pad) * dw_ref[k:k + 1, :]
    mu = jnp.mean(conv, axis=-1, keepdims=True)
    cen = conv - mu
    var = jnp.mean(cen * cen, axis=-1, keepdims=True)
    hln = cen * lax.rsqrt(var + EPS) * lng_ref[...] + lnb_ref[...]
    hact = hln * jax.nn.sigmoid(hln)
    y_conv = jnp.dot(hact.astype(BF16), pw_ref[...], preferred_element_type=F32)

    def ch(usv):
        return usv[:, 2 * gw:] * usv[:, :gw]

    fill(ext_sc, ch(us_prev[...]), ch(us_ref[...]), ch(us_next[...]))
    spad = SHORT_CONV_WIDTH // 2
    sconv = jnp.zeros((ts, gw), F32)
    for k in range(SHORT_CONV_WIDTH):
        sconv = sconv + shifted(ext_sc, k - spad) * sw_ref[k:k + 1, :]
    y_sc = us_ref[:, gw:2 * gw] * sconv

    qn = (_seg_rms(um_ref[...], gqm_ref[...], seg_ref[...]) * (MEM_HEAD_DIM ** -0.5)).astype(BF16)
    kmt = kmt_ref[0]
    vm = vm_ref[0]
    y_mem = jnp.zeros((ts, gw), F32)
    for h in range(MEM_HEADS):
        hmask = _lane_range_mask((ts, gw), h * MEM_HEAD_DIM, (h + 1) * MEM_HEAD_DIM)
        qm = jnp.where(hmask, qn, jnp.zeros_like(qn))
        s = jnp.dot(qm, kmt, preferred_element_type=F32)
        e = jnp.exp(s - jnp.max(s, axis=-1, keepdims=True))
        p = (e * (1.0 / jnp.sum(e, axis=-1, keepdims=True))).astype(BF16)
        y_mem = jnp.where(hmask, jnp.dot(p, vm, preferred_element_type=F32), y_mem)

    acc = x_ref[...]
    for gi, y in enumerate((y_pool, y_conv, yd_ref[...], y_sc, y_mem)):
        yn = _rms(y, gn_ref[gi:gi + 1, :]).astype(BF16)
        acc = acc + jnp.dot(yn, wout_ref[gi * gw:(gi + 1) * gw, :], preferred_element_type=F32)
    o_ref[...] = acc


def _mixers(x2d, upool, uconv, usc, umem, ydiff, kmt, vm, poolw, pools, dw, lng, lnb, pw, sw, gqm, seg64,
            gn, wout, batch, seq):
    t = x2d.shape[0]
    gw = GROUP_WIDTH
    ts = min(TS_MIX, seq)
    nst = seq // ts
    r = ts // HALO
    last = t // HALO - 1
    row = lambda c: pl.BlockSpec((ts, c), lambda i: (i, 0))
    prev = lambda c: pl.BlockSpec((HALO, c), lambda i: (jnp.maximum(i * r - 1, 0), 0))
    nxt = lambda c: pl.BlockSpec((HALO, c), lambda i: (jnp.minimum((i + 1) * r, last), 0))
    full = lambda a: pl.BlockSpec(a.shape, lambda i: (0,) * a.ndim)
    perb = lambda a: pl.BlockSpec((1,) + a.shape[1:], lambda i: (i // nst, 0, 0))
    consts = (poolw, pools, dw, lng, lnb, pw, sw, gqm, seg64, gn, wout)
    return pl.pallas_call(
        functools.partial(_mixer_kernel, seq=seq),
        grid=(t // ts,),
        in_specs=[row(D_MODEL),
                  row(gw), prev(gw), nxt(gw),
                  row(2 * gw), prev(2 * gw), nxt(2 * gw),
                  row(3 * gw), prev(3 * gw), nxt(3 * gw),
                  row(gw), row(gw), perb(kmt), perb(vm)] + [full(a) for a in consts],
        out_specs=row(D_MODEL),
        out_shape=jax.ShapeDtypeStruct((t, D_MODEL), F32),
        scratch_shapes=[pltpu.VMEM((ts + 2 * HALO, gw), F32)] * 3,
        compiler_params=_params(("parallel",)),
        name="mixers_out_projection",
    )(x2d, upool, upool, upool, uconv, uconv, uconv, usc, usc, usc, umem, ydiff, kmt, vm, *consts)


def _top_values(s, n):
    vals = []
    for _ in range(n):
        m = jnp.max(s, axis=0, keepdims=True)
        vals.append(m)
        s = jnp.where(s == m, NEG, s)
    return vals


def _retrieval_kernel(x_ref, g_ref, wqt_ref, sk_ref, xnt_ref, thr_ref, e1_ref, e2_ref):
    tm = x_ref.shape[0]
    nk = PEER_N_KEYS
    n = PEER_TOPK + 1
    xb = _rms(x_ref[...], g_ref[...]).T.astype(BF16)
    xnt_ref[...] = xb
    qt = jnp.dot(wqt_ref[...], xb, preferred_element_type=F32)
    row_id = lax.broadcasted_iota(jnp.int32, (8, tm), 0)
    for h in range(PEER_HEADS):
        sc = []
        for j in range(2):
            r0 = (2 * h + j) * nk
            sc.append(jnp.dot(sk_ref[2 * h + j], qt[r0:r0 + nk].astype(BF16), preferred_element_type=F32))
        v1 = _top_values(sc[0], n)
        v2 = _top_values(sc[1], n)
        v2_blocks = []
        for r0 in range(0, n, 8):
            blk = jnp.full((8, tm), NEG, F32)
            for i in range(r0, min(r0 + 8, n)):
                blk = jnp.where(row_id == i - r0, v2[i], blk)
            v2_blocks.append(blk)
        blocks = [v1[0] + blk for blk in v2_blocks]
        for i in range(1, n):
            blocks.append(v1[i] + jnp.where(row_id < n // (i + 1), v2_blocks[0], NEG))
        best = _top_values(jnp.concatenate(blocks, axis=0), n)
        z = jnp.zeros((1, tm), F32)
        for i in range(PEER_TOPK):
            z = z + jnp.exp(best[i] - best[0])
        half_inv_z = 0.5 / z
        tau = 0.5 * (best[PEER_TOPK - 1] + best[PEER_TOPK])
        thr = jnp.exp(tau - sc[0] - v2[0]) * half_inv_z
        e1 = jnp.exp(sc[0] - v1[0])
        e2 = jnp.exp(sc[1] - v2[0]) * half_inv_z
        for c in range(tm // LANES):
            cs = slice(c * LANES, (c + 1) * LANES)
            thr_ref[h, c] = thr[:, cs]
            e1_ref[h, c] = e1[:, cs]
            e2_ref[h, c] = e2[:, cs]


def _retrieval(x2d, g, wqt, sk):
    t = x2d.shape[0]
    tm = min(TM_RET, t)
    nk = PEER_N_KEYS
    full = lambda a: pl.BlockSpec(a.shape, lambda i: (0,) * a.ndim)
    tab = pl.BlockSpec((PEER_HEADS, tm // LANES, nk, LANES), lambda i: (0, i, 0, 0))
    tab_shape = jax.ShapeDtypeStruct((PEER_HEADS, t // LANES, nk, LANES), F32)
    return pl.pallas_call(
        _retrieval_kernel,
        grid=(t // tm,),
        in_specs=[pl.BlockSpec((tm, D_MODEL), lambda i: (i, 0)), full(g), full(wqt), full(sk)],
        out_specs=[pl.BlockSpec((D_MODEL, tm), lambda i: (0, i)), tab, tab, tab],
        out_shape=[jax.ShapeDtypeStruct((D_MODEL, t), BF16), tab_shape, tab_shape, tab_shape],
        compiler_params=_params(("parallel",)),
        name="peer_retrieval",
    )(x2d, g, wqt, sk)


def _expert_kernel(x_ref, xnt_ref, u_ref, vt_ref, thr_ref, e1_ref, e2_ref, o_ref, acc_ref, w_ref):
    nk = PEER_N_KEYS
    eb = pl.program_id(1)

    @pl.when(eb == 0)
    def _():
        acc_ref[...] = jnp.zeros_like(acc_ref)

    tm = xnt_ref.shape[1]
    ht = jnp.dot(u_ref[...], xnt_ref[...], preferred_element_type=F32)
    for al in range(A_BLOCK):
        for c in range(tm // LANES):
            cs = slice(c * LANES, (c + 1) * LANES)
            hs = ht[al * nk:(al + 1) * nk, cs]
            gate = None
            for h in range(PEER_HEADS):
                e2 = e2_ref[h, c]
                sel = jnp.where(e2 >= thr_ref[h, c, al:al + 1, :], e2, 0.0)
                term = e1_ref[h, c, al:al + 1, :] * sel
                gate = term if gate is None else gate + term
            w_ref[al * nk:(al + 1) * nk, cs] = (gate * (hs * (1.0 + lax.erf(hs * math.sqrt(0.5))))).astype(BF16)
    acc_ref[...] += jnp.dot(vt_ref[...], w_ref[...], preferred_element_type=F32)

    @pl.when(eb == pl.num_programs(1) - 1)
    def _():
        o_ref[...] = x_ref[...] + acc_ref[...].T


def _experts(x2d, xnt, u, vt, thr, e1, e2):
    t = x2d.shape[0]
    tm = min(TM_EXP, t)
    nk = PEER_N_KEYS
    te = A_BLOCK * nk
    nc = tm // LANES
    tab_a = pl.BlockSpec((PEER_HEADS, nc, A_BLOCK, LANES), lambda i, e: (0, i, e, 0))
    tab_b = pl.BlockSpec((PEER_HEADS, nc, nk, LANES), lambda i, e: (0, i, 0, 0))
    return pl.pallas_call(
        _expert_kernel,
        grid=(t // tm, nk // A_BLOCK),
        in_specs=[pl.BlockSpec((tm, D_MODEL), lambda i, e: (i, 0)),
                  pl.BlockSpec((D_MODEL, tm), lambda i, e: (0, i)),
                  pl.BlockSpec((te, D_MODEL), lambda i, e: (e, 0)),
                  pl.BlockSpec((D_MODEL, te), lambda i, e: (0, e)),
                  tab_a, tab_a, tab_b],
        out_specs=pl.BlockSpec((tm, D_MODEL), lambda i, e: (i, 0)),
        out_shape=jax.ShapeDtypeStruct((t, D_MODEL), F32),
        scratch_shapes=[pltpu.VMEM((D_MODEL, tm), F32), pltpu.VMEM((te, tm), BF16)],
        compiler_params=_params(("parallel", "arbitrary")),
        name="peer_experts",
    )(x2d, xnt, u, vt, thr, e1, e2)


def _segment_mean_matrix(width, seg):
    idx = jnp.arange(width) // seg
    return (idx[:, None] == idx[None, :]).astype(F32) / seg


def _block_diag(w):
    g, c, _ = w.shape
    eye = jnp.eye(g, dtype=w.dtype)
    return (eye[:, None, :, None] * w[:, :, None, :]).reshape(g * c, g * c)


def kernel(x, mem, norm_mix, w_in, pool_w, pool_scale, conv_dw, conv_ln_g, conv_ln_b, conv_pw, diff_qk_norm,
           diff_lambda, diff_subln, sconv_w, mem_norm, w_mem_kv, mem_qk_norm, group_norm, w_out, norm_ffn,
           peer_wq, peer_subkeys, peer_u, peer_v):
    b, s, d = x.shape
    depth = w_in.shape[0]
    gw = GROUP_WIDTH
    seg32 = _segment_mean_matrix(gw, DIFF_HEAD_DIM)
    seg64 = _segment_mean_matrix(gw, MEM_HEAD_DIM)
    row = lambda a: a.reshape(1, -1)
    tile = lambda a, n: jnp.tile(a, n).reshape(1, -1)
    x2d = x.reshape(b * s, d)
    for l in range(depth):
        lam_init = jnp.full((1, 1), 0.8 - 0.6 * math.exp(-0.3 * l), F32)
        upool, uconv, usc, umem, q, kt, v = _inproj(
            x2d, row(norm_mix[l]), w_in[l].astype(BF16), tile(diff_qk_norm[l, 0], 2 * DIFF_HEADS),
            tile(diff_qk_norm[l, 1], 2 * DIFF_HEADS), seg32, b, s)
        kmt, vm = _memkv(mem, row(mem_norm[l]), w_mem_kv[l].astype(BF16), tile(mem_qk_norm[l, 1], MEM_HEADS), seg64)
        ydiff = _diffattn(q, kt, v, diff_lambda[l], lam_init, tile(diff_subln[l], DIFF_HEADS), seg64, b, s)
        x2d = _mixers(x2d, upool, uconv, usc, umem, ydiff, kmt, vm,
                      _block_diag(pool_w[l]).astype(BF16), row(pool_scale[l]), conv_dw[l], row(conv_ln_g[l]),
                      row(conv_ln_b[l]), conv_pw[l].astype(BF16), sconv_w[l], tile(mem_qk_norm[l, 0], MEM_HEADS),
                      seg64, group_norm[l].reshape(N_GROUPS, gw), w_out[l].astype(BF16), b, s)
        xnt, thr, e1, e2 = _retrieval(
            x2d, row(norm_ffn[l]), peer_wq[l].T.astype(BF16),
            peer_subkeys[l].reshape(2 * PEER_HEADS, PEER_N_KEYS, -1).astype(BF16))
        x2d = _experts(x2d, xnt, peer_u[l].astype(BF16), peer_v[l].T.astype(BF16), thr, e1, e2)
    return x2d.reshape(b, s, d)
```

```python
import functools
import math

import jax
import jax.numpy as jnp
from jax import lax
from jax.experimental import pallas as pl
from jax.experimental.pallas import tpu as pltpu

F32 = jnp.float32
BF16 = jnp.bfloat16

D_MODEL = 1024
GROUP_WIDTH = 256
N_GROUPS = 5
POOL_HALF_WINDOWS = (1, 2, 4, 8)
POOL_CH = 64
CONV_WIDTH = 31
SHORT_CONV_WIDTH = 3
DIFF_HEADS = 4
DIFF_HEAD_DIM = 32
MEM_HEADS = 4
MEM_HEAD_DIM = 64
PEER_HEADS = 8
PEER_N_KEYS = 128
PEER_TOPK = 16
EPS = 1e-6

LANES = 128
HALO = 16
VMEM_LIMIT = 56 * 1024 * 1024

TM_IN = 512
TQ_ATT = 256
TS_MIX = 512
TM_RET = 256
TM_EXP = 512
A_BLOCK = 8
NEG = -1e30
LOG2E = 1.4426950408889634


def _params(sem):
    return pltpu.CompilerParams(dimension_semantics=sem, vmem_limit_bytes=VMEM_LIMIT)


def _rms(x, g):
    return x * lax.rsqrt(jnp.mean(x * x, axis=-1, keepdims=True) + EPS) * g


def _seg_rms(x, g, seg_mean):
    ms = jnp.dot(x * x, seg_mean, precision=lax.Precision.HIGHEST, preferred_element_type=F32)
    return x * lax.rsqrt(ms + EPS) * g


def _lane_range_mask(shape, lo, hi):
    lane = lax.broadcasted_iota(jnp.int32, shape, len(shape) - 1)
    return (lane >= lo) & (lane < hi)


def _inproj_kernel(x_ref, g_ref, w_ref, gq_ref, gk_ref, seg_ref,
                   upool_ref, uconv_ref, usc_ref, umem_ref, q_ref, kt_ref, v_ref):
    xb = _rms(x_ref[...], g_ref[...]).astype(BF16)

    def proj(lo, hi):
        return jnp.dot(xb, w_ref[:, lo:hi], preferred_element_type=F32)

    gw = GROUP_WIDTH
    upool_ref[...] = proj(0, gw)
    uconv_ref[...] = proj(gw, 3 * gw)
    q = proj(3 * gw, 4 * gw)
    k = proj(4 * gw, 5 * gw)
    v_ref[...] = proj(5 * gw, 6 * gw).astype(BF16)
    usc_ref[...] = proj(6 * gw, 9 * gw)
    umem_ref[...] = proj(9 * gw, 10 * gw)
    seg = seg_ref[...]
    q_ref[...] = (_seg_rms(q, gq_ref[...], seg) * (DIFF_HEAD_DIM ** -0.5 * LOG2E)).astype(BF16)
    kt_ref[0] = _seg_rms(k, gk_ref[...], seg).T.astype(BF16)


def _inproj(x2d, g, w, gq, gk, seg32, batch, seq):
    t = x2d.shape[0]
    tm = min(TM_IN, seq)
    nst = seq // tm
    gw = GROUP_WIDTH
    row = lambda c: pl.BlockSpec((tm, c), lambda i: (i, 0))
    full = lambda a: pl.BlockSpec(a.shape, lambda i: (0,) * a.ndim)
    return pl.pallas_call(
        _inproj_kernel,
        grid=(t // tm,),
        in_specs=[row(D_MODEL), full(g), full(w), full(gq), full(gk), full(seg32)],
        out_specs=[row(gw), row(2 * gw), row(3 * gw), row(gw), row(gw),
                   pl.BlockSpec((1, gw, tm), lambda i: (i // nst, 0, i % nst)), row(gw)],
        out_shape=[jax.ShapeDtypeStruct((t, gw), F32), jax.ShapeDtypeStruct((t, 2 * gw), F32),
                   jax.ShapeDtypeStruct((t, 3 * gw), F32), jax.ShapeDtypeStruct((t, gw), F32),
                   jax.ShapeDtypeStruct((t, gw), BF16), jax.ShapeDtypeStruct((batch, gw, seq), BF16),
                   jax.ShapeDtypeStruct((t, gw), BF16)],
        compiler_params=_params(("parallel",)),
        name="in_projection",
    )(x2d, g, w, gq, gk, seg32)


def _memkv_kernel(mem_ref, g_ref, w_ref, gk_ref, seg_ref, kt_ref, v_ref):
    mn = _rms(mem_ref[0], g_ref[...]).astype(BF16)
    kv = jnp.dot(mn, w_ref[...], preferred_element_type=F32)
    k = _seg_rms(kv[:, :GROUP_WIDTH], gk_ref[...], seg_ref[...])
    kt_ref[0] = k.T.astype(BF16)
    v_ref[0] = kv[:, GROUP_WIDTH:].astype(BF16)


def _memkv(mem, g, w, gk, seg64):
    b, m, _ = mem.shape
    gw = GROUP_WIDTH
    full = lambda a: pl.BlockSpec(a.shape, lambda i: (0,) * a.ndim)
    return pl.pallas_call(
        _memkv_kernel,
        grid=(b,),
        in_specs=[pl.BlockSpec((1, m, D_MODEL), lambda i: (i, 0, 0)), full(g), full(w), full(gk), full(seg64)],
        out_specs=[pl.BlockSpec((1, gw, m), lambda i: (i, 0, 0)), pl.BlockSpec((1, m, gw), lambda i: (i, 0, 0))],
        out_shape=[jax.ShapeDtypeStruct((b, gw, m), BF16), jax.ShapeDtypeStruct((b, m, gw), BF16)],
        compiler_params=_params(("parallel",)),
        name="memory_kv",
    )(mem, g, w, gk, seg64)


def _diffattn_kernel(q_ref, kt_ref, v_ref, lam_ref, laminit_ref, gsub_ref, seg_ref, o_ref):
    tq = q_ref.shape[0]
    seq = kt_ref.shape[2]
    lp = lam_ref[...]
    lam_init = laminit_ref[...]
    lam = (jnp.exp(jnp.sum(lp[0:1] * lp[1:2], axis=-1, keepdims=True))
           - jnp.exp(jnp.sum(lp[2:3] * lp[3:4], axis=-1, keepdims=True)) + lam_init)
    qpos = pl.program_id(1) * tq + lax.broadcasted_iota(jnp.int32, (tq, seq), 0)
    kpos = lax.broadcasted_iota(jnp.int32, (tq, seq), 1)
    dist = jnp.abs(qpos - kpos).astype(F32)
    q = q_ref[...]
    kt = kt_ref[0]
    v = v_ref[...]
    o = jnp.zeros((tq, GROUP_WIDTH), F32)
    half = DIFF_HEADS * DIFF_HEAD_DIM
    for h in range(DIFF_HEADS):
        slope = 2.0 ** (-8.0 * (h + 1) / DIFF_HEADS)
        bias = (slope * LOG2E) * dist
        outs = []
        for j in range(2):
            lo = j * half + h * DIFF_HEAD_DIM
            qm = jnp.where(_lane_range_mask(q.shape, lo, lo + DIFF_HEAD_DIM), q, jnp.zeros_like(q))
            s = jnp.dot(qm, kt, preferred_element_type=F32) - bias
            e = jnp.exp2(s - jnp.max(s, axis=-1, keepdims=True))
            inv_z = 1.0 / jnp.sum(e, axis=-1, keepdims=True)
            outs.append(jnp.dot(e.astype(BF16), v, preferred_element_type=F32) * inv_z)
        oh = outs[0] - lam * outs[1]
        vlo = h * 2 * DIFF_HEAD_DIM
        o = jnp.where(_lane_range_mask(o.shape, vlo, vlo + 2 * DIFF_HEAD_DIM), oh, o)
    o_ref[...] = _seg_rms(o, gsub_ref[...], seg_ref[...]) * (1.0 - lam_init)


def _diffattn(q, kt, v, lam_p, lam_init, gsub, seg64, batch, seq):
    t = q.shape[0]
    gw = GROUP_WIDTH
    tq = min(TQ_ATT, seq)
    nq = seq // tq
    full = lambda a: pl.BlockSpec(a.shape, lambda b, i: (0,) * a.ndim)
    return pl.pallas_call(
        _diffattn_kernel,
        grid=(batch, nq),
        in_specs=[pl.BlockSpec((tq, gw), lambda b, i: (b * nq + i, 0)),
                  pl.BlockSpec((1, gw, seq), lambda b, i: (b, 0, 0)),
                  pl.BlockSpec((seq, gw), lambda b, i: (b, 0)),
                  full(lam_p), full(lam_init), full(gsub), full(seg64)],
        out_specs=pl.BlockSpec((tq, gw), lambda b, i: (b * nq + i, 0)),
        out_shape=jax.ShapeDtypeStruct((t, gw), F32),
        compiler_params=_params(("parallel", "parallel")),
        name="diff_attention",
    )(q, kt, v, lam_p, lam_init, gsub, seg64)


def _mixer_kernel(x_ref, up_ref, up_prev, up_next, uc_ref, uc_prev, uc_next, us_ref, us_prev, us_next,
                  um_ref, yd_ref, kmt_ref, vm_ref,
                  poolw_ref, pools_ref, dw_ref, lng_ref, lnb_ref, pw_ref, sw_ref, gqm_ref, seg_ref,
                  gn_ref, wout_ref, o_ref, ext_pool, ext_conv, ext_sc, *, seq):
    ts = x_ref.shape[0]
    gw = GROUP_WIDTH
    nst = seq // ts
    it = pl.program_id(0) % nst
    keep_prev = jnp.where(it > 0, 1.0, 0.0)
    keep_next = jnp.where(it < nst - 1, 1.0, 0.0)

    def fill(ext, prev, main, nxt):
        ext[0:HALO, :] = prev * keep_prev
        ext[HALO:HALO + ts, :] = main
        ext[HALO + ts:HALO + ts + HALO, :] = nxt * keep_next

    def shifted(ext, k):
        return ext[HALO + k:HALO + k + ts, :]

    fill(ext_pool, up_prev[...], up_ref[...], up_next[...])
    u = up_ref[...]
    lane = lax.broadcasted_iota(jnp.int32, (ts, gw), 1)
    pos = it * ts + lax.broadcasted_iota(jnp.int32, (ts, gw), 0)
    wsum = jnp.zeros((ts, gw), F32)
    halfw = jnp.zeros((ts, gw), jnp.int32)
    run = None
    prev_half = 0
    for gi, hw in enumerate(POOL_HALF_WINDOWS):
        for k in list(range(-hw, -prev_half)) + list(range(prev_half, hw)):
            term = shifted(ext_pool, k)
            run = term if run is None else run + term
        prev_half = hw
        in_group = (lane >= gi * POOL_CH) & (lane < (gi + 1) * POOL_CH)
        wsum = jnp.where(in_group, run, wsum)
        halfw = jnp.where(in_group, hw, halfw)
    cnt = (jnp.minimum(pos + halfw, seq) - jnp.maximum(pos - halfw, 0)).astype(F32)
    pooled = wsum / cnt - u
    y_pool = jnp.dot(pooled.astype(BF16), poolw_ref[...], preferred_element_type=F32) * pools_ref[...]

    def glu(ucv):
        return ucv[:, :gw] * jax.nn.sigmoid(ucv[:, gw:])

    fill(ext_conv, glu(uc_prev[...]), glu(uc_ref[...]), glu(uc_next[...]))
    pad = CONV_WIDTH // 2
    conv = jnp.zeros((ts, gw), F32)
    for k in range(CONV_WIDTH):
        conv = conv + shifted(ext_conv, k - pad) * dw_ref[k:k + 1, :]
    mu = jnp.mean(conv, axis=-1, keepdims=True)
    cen = conv - mu
    var = jnp.mean(cen * cen, axis=-1, keepdims=True)
    hln = cen * lax.rsqrt(var + EPS) * lng_ref[...] + lnb_ref[...]
    hact = hln * jax.nn.sigmoid(hln)
    y_conv = jnp.dot(hact.astype(BF16), pw_ref[...], preferred_element_type=F32)

    def ch(usv):
        return usv[:, 2 * gw:] * usv[:, :gw]

    fill(ext_sc, ch(us_prev[...]), ch(us_ref[...]), ch(us_next[...]))
    spad = SHORT_CONV_WIDTH // 2
    sconv = jnp.zeros((ts, gw), F32)
    for k in range(SHORT_CONV_WIDTH):
        sconv = sconv + shifted(ext_sc, k - spad) * sw_ref[k:k + 1, :]
    y_sc = us_ref[:, gw:2 * gw] * sconv

    qn = (_seg_rms(um_ref[...], gqm_ref[...], seg_ref[...]) * (MEM_HEAD_DIM ** -0.5)).astype(BF16)
    kmt = kmt_ref[0]
    vm = vm_ref[0]
    y_mem = jnp.zeros((ts, gw), F32)
    for h in range(MEM_HEADS):
        hmask = _lane_range_mask((ts, gw), h * MEM_HEAD_DIM, (h + 1) * MEM_HEAD_DIM)
        qm = jnp.where(hmask, qn, jnp.zeros_like(qn))
        s = jnp.dot(qm, kmt, preferred_element_type=F32)
        e = jnp.exp(s - jnp.max(s, axis=-1, keepdims=True))
        p = (e * (1.0 / jnp.sum(e, axis=-1, keepdims=True))).astype(BF16)
        y_mem = jnp.where(hmask, jnp.dot(p, vm, preferred_element_type=F32), y_mem)

    acc = x_ref[...]
    for gi, y in enumerate((y_pool, y_conv, yd_ref[...], y_sc, y_mem)):
        yn = _rms(y, gn_ref[gi:gi + 1, :]).astype(BF16)
        acc = acc + jnp.dot(yn, wout_ref[gi * gw:(gi + 1) * gw, :], preferred_element_type=F32)
    o_ref[...] = acc


def _mixers(x2d, upool, uconv, usc, umem, ydiff, kmt, vm, poolw, pools, dw, lng, lnb, pw, sw, gqm, seg64,
            gn, wout, batch, seq):
    t = x2d.shape[0]
    gw = GROUP_WIDTH
    ts = min(TS_MIX, seq)
    nst = seq // ts
    r = ts // HALO
    last = t // HALO - 1
    row = lambda c: pl.BlockSpec((ts, c), lambda i: (i, 0))
    prev = lambda c: pl.BlockSpec((HALO, c), lambda i: (jnp.maximum(i * r - 1, 0), 0))
    nxt = lambda c: pl.BlockSpec((HALO, c), lambda i: (jnp.minimum((i + 1) * r, last), 0))
    full = lambda a: pl.BlockSpec(a.shape, lambda i: (0,) * a.ndim)
    perb = lambda a: pl.BlockSpec((1,) + a.shape[1:], lambda i: (i // nst, 0, 0))
    consts = (poolw, pools, dw, lng, lnb, pw, sw, gqm, seg64, gn, wout)
    return pl.pallas_call(
        functools.partial(_mixer_kernel, seq=seq),
        grid=(t // ts,),
        in_specs=[row(D_MODEL),
                  row(gw), prev(gw), nxt(gw),
                  row(2 * gw), prev(2 * gw), nxt(2 * gw),
                  row(3 * gw), prev(3 * gw), nxt(3 * gw),
                  row(gw), row(gw), perb(kmt), perb(vm)] + [full(a) for a in consts],
        out_specs=row(D_MODEL),
        out_shape=jax.ShapeDtypeStruct((t, D_MODEL), F32),
        scratch_shapes=[pltpu.VMEM((ts + 2 * HALO, gw), F32)] * 3,
        compiler_params=_params(("parallel",)),
        name="mixers_out_projection",
    )(x2d, upool, upool, upool, uconv, uconv, uconv, usc, usc, usc, umem, ydiff, kmt, vm, *consts)


def _sorting_network(n):
    pairs = []
    p = 1
    while p < n:
        k = p
        while k >= 1:
            for j in range(k % p, n - k, 2 * k):
                for i in range(min(k, n - j - k)):
                    if (i + j) // (2 * p) == (i + j + k) // (2 * p):
                        pairs.append((i + j, i + j + k))
            k //= 2
        p *= 2
    return pairs


def _sort_blocks_descending(blocks):
    blocks = list(blocks)
    for i, j in _sorting_network(len(blocks)):
        hi, lo = jnp.maximum(blocks[i], blocks[j]), jnp.minimum(blocks[i], blocks[j])
        blocks[i], blocks[j] = hi, lo
    return blocks


def _pop_top_values(stack, singles, n):
    stack, singles = list(stack), list(singles)
    vals = []
    for k in range(n):
        top = stack[0]
        for blk in singles:
            top = jnp.maximum(top, blk)
        m = jnp.max(top, axis=0, keepdims=True)
        vals.append(m)
        keep = min(len(stack), n - k - 1)
        if keep == 0:
            break
        hit = stack[0] == m
        stack = [jnp.where(hit, stack[r + 1] if r + 1 < len(stack) else NEG, stack[r]) for r in range(keep)]
        singles = [jnp.where(blk == m, NEG, blk) for blk in singles]
    return vals


def _retrieval_kernel(x_ref, g_ref, wqt_ref, sk_ref, xnt_ref, thr_ref, e1_ref, e2_ref):
    tm = x_ref.shape[0]
    nk = PEER_N_KEYS
    n = PEER_TOPK + 1
    xb = _rms(x_ref[...], g_ref[...]).T.astype(BF16)
    xnt_ref[...] = xb
    qt = jnp.dot(wqt_ref[...], xb, preferred_element_type=F32)
    row_id = lax.broadcasted_iota(jnp.int32, (8, tm), 0)
    for h in range(PEER_HEADS):
        sc = []
        for j in range(2):
            r0 = (2 * h + j) * nk
            sc.append(jnp.dot(sk_ref[2 * h + j], qt[r0:r0 + nk].astype(BF16), preferred_element_type=F32))
        v1, v2 = (_pop_top_values(_sort_blocks_descending([s[r:r + 8] for r in range(0, nk, 8)]), [], n)
                  for s in sc)
        v2_blocks = []
        for r0 in range(0, n, 8):
            blk = jnp.full((8, tm), NEG, F32)
            for i in range(r0, min(r0 + 8, n)):
                blk = jnp.where(row_id == i - r0, v2[i], blk)
            v2_blocks.append(blk)
        stack = [v1[0] + v2_blocks[0]]
        for i in range(1, n):
            stack.append(v1[i] + jnp.where(row_id < n // (i + 1), v2_blocks[0], NEG))
        best = _pop_top_values(stack, [v1[0] + blk for blk in v2_blocks[1:]], n)
        z = jnp.zeros((1, tm), F32)
        for i in range(PEER_TOPK):
            z = z + jnp.exp(best[i] - best[0])
        half_inv_z = 0.5 / z
        tau = 0.5 * (best[PEER_TOPK - 1] + best[PEER_TOPK])
        thr = jnp.exp(tau - sc[0] - v2[0]) * half_inv_z
        e1 = jnp.exp(sc[0] - v1[0])
        e2 = jnp.exp(sc[1] - v2[0]) * half_inv_z
        for c in range(tm // LANES):
            cs = slice(c * LANES, (c + 1) * LANES)
            thr_ref[h, c] = thr[:, cs]
            e1_ref[h, c] = e1[:, cs]
            e2_ref[h, c] = e2[:, cs]


def _retrieval(x2d, g, wqt, sk):
    t = x2d.shape[0]
    tm = min(TM_RET, t)
    nk = PEER_N_KEYS
    full = lambda a: pl.BlockSpec(a.shape, lambda i: (0,) * a.ndim)
    tab = pl.BlockSpec((PEER_HEADS, tm // LANES, nk, LANES), lambda i: (0, i, 0, 0))
    tab_shape = jax.ShapeDtypeStruct((PEER_HEADS, t // LANES, nk, LANES), F32)
    return pl.pallas_call(
        _retrieval_kernel,
        grid=(t // tm,),
        in_specs=[pl.BlockSpec((tm, D_MODEL), lambda i: (i, 0)), full(g), full(wqt), full(sk)],
        out_specs=[pl.BlockSpec((D_MODEL, tm), lambda i: (0, i)), tab, tab, tab],
        out_shape=[jax.ShapeDtypeStruct((D_MODEL, t), BF16), tab_shape, tab_shape, tab_shape],
        compiler_params=_params(("parallel",)),
        name="peer_retrieval",
    )(x2d, g, wqt, sk)


def _expert_kernel(x_ref, xnt_ref, u_ref, vt_ref, thr_ref, e1_ref, e2_ref, o_ref, acc_ref, w_ref):
    nk = PEER_N_KEYS
    eb = pl.program_id(1)

    @pl.when(eb == 0)
    def _():
        acc_ref[...] = jnp.zeros_like(acc_ref)

    tm = xnt_ref.shape[1]
    ht = jnp.dot(u_ref[...], xnt_ref[...], preferred_element_type=F32)
    for al in range(A_BLOCK):
        for c in range(tm // LANES):
            cs = slice(c * LANES, (c + 1) * LANES)
            hs = ht[al * nk:(al + 1) * nk, cs]
            gate = None
            for h in range(PEER_HEADS):
                e2 = e2_ref[h, c]
                sel = jnp.where(e2 >= thr_ref[h, c, al:al + 1, :], e2, 0.0)
                term = e1_ref[h, c, al:al + 1, :] * sel
                gate = term if gate is None else gate + term
            w_ref[al * nk:(al + 1) * nk, cs] = (gate * (hs * (1.0 + lax.erf(hs * math.sqrt(0.5))))).astype(BF16)
    acc_ref[...] += jnp.dot(vt_ref[...], w_ref[...], preferred_element_type=F32)

    @pl.when(eb == pl.num_programs(1) - 1)
    def _():
        o_ref[...] = x_ref[...] + acc_ref[...].T


def _experts(x2d, xnt, u, vt, thr, e1, e2):
    t = x2d.shape[0]
    tm = min(TM_EXP, t)
    nk = PEER_N_KEYS
    te = A_BLOCK * nk
    nc = tm // LANES
    tab_a = pl.BlockSpec((PEER_HEADS, nc, A_BLOCK, LANES), lambda i, e: (0, i, e, 0))
    tab_b = pl.BlockSpec((PEER_HEADS, nc, nk, LANES), lambda i, e: (0, i, 0, 0))
    return pl.pallas_call(
        _expert_kernel,
        grid=(t // tm, nk // A_BLOCK),
        in_specs=[pl.BlockSpec((tm, D_MODEL), lambda i, e: (i, 0)),
                  pl.BlockSpec((D_MODEL, tm), lambda i, e: (0, i)),
                  pl.BlockSpec((te, D_MODEL), lambda i, e: (e, 0)),
                  pl.BlockSpec((D_MODEL, te), lambda i, e: (0, e)),
                  tab_a, tab_a, tab_b],
        out_specs=pl.BlockSpec((tm, D_MODEL), lambda i, e: (i, 0)),
        out_shape=jax.ShapeDtypeStruct((t, D_MODEL), F32),
        scratch_shapes=[pltpu.VMEM((D_MODEL, tm), F32), pltpu.VMEM((te, tm), BF16)],
        compiler_params=_params(("parallel", "arbitrary")),
        name="peer_experts",
    )(x2d, xnt, u, vt, thr, e1, e2)


def _segment_mean_matrix(width, seg):
    idx = jnp.arange(width) // seg
    return (idx[:, None] == idx[None, :]).astype(F32) / seg


def _block_diag(w):
    g, c, _ = w.shape
    eye = jnp.eye(g, dtype=w.dtype)
    return (eye[:, None, :, None] * w[:, :, None, :]).reshape(g * c, g * c)


def kernel(x, mem, norm_mix, w_in, pool_w, pool_scale, conv_dw, conv_ln_g, conv_ln_b, conv_pw, diff_qk_norm,
           diff_lambda, diff_subln, sconv_w, mem_norm, w_mem_kv, mem_qk_norm, group_norm, w_out, norm_ffn,
           peer_wq, peer_subkeys, peer_u, peer_v):
    b, s, d = x.shape
    depth = w_in.shape[0]
    gw = GROUP_WIDTH
    seg32 = _segment_mean_matrix(gw, DIFF_HEAD_DIM)
    seg64 = _segment_mean_matrix(gw, MEM_HEAD_DIM)
    row = lambda a: a.reshape(1, -1)
    tile = lambda a, n: jnp.tile(a, n).reshape(1, -1)
    x2d = x.reshape(b * s, d)
    for l in range(depth):
        lam_init = jnp.full((1, 1), 0.8 - 0.6 * math.exp(-0.3 * l), F32)
        upool, uconv, usc, umem, q, kt, v = _inproj(
            x2d, row(norm_mix[l]), w_in[l].astype(BF16), tile(diff_qk_norm[l, 0], 2 * DIFF_HEADS),
            tile(diff_qk_norm[l, 1], 2 * DIFF_HEADS), seg32, b, s)
        kmt, vm = _memkv(mem, row(mem_norm[l]), w_mem_kv[l].astype(BF16), tile(mem_qk_norm[l, 1], MEM_HEADS), seg64)
        ydiff = _diffattn(q, kt, v, diff_lambda[l], lam_init, tile(diff_subln[l], DIFF_HEADS), seg64, b, s)
        x2d = _mixers(x2d, upool, uconv, usc, umem, ydiff, kmt, vm,
                      _block_diag(pool_w[l]).astype(BF16), row(pool_scale[l]), conv_dw[l], row(conv_ln_g[l]),
                      row(conv_ln_b[l]), conv_pw[l].astype(BF16), sconv_w[l], tile(mem_qk_norm[l, 0], MEM_HEADS),
                      seg64, group_norm[l].reshape(N_GROUPS, gw), w_out[l].astype(BF16), b, s)
        xnt, thr, e1, e2 = _retrieval(
            x2d, row(norm_ffn[l]), peer_wq[l].T.astype(BF16),
            peer_subkeys[l].reshape(2 * PEER_HEADS, PEER_N_KEYS, -1).astype(BF16))
        x2d = _experts(x2d, xnt, peer_u[l].astype(BF16), peer_v[l].T.astype(BF16), thr, e1, e2)
    return x2d.reshape(b, s, d)
```

```python
import functools
import math

import jax
import jax.numpy as jnp
from jax import lax
from jax.experimental import pallas as pl
from jax.experimental.pallas import tpu as pltpu

F32 = jnp.float32
BF16 = jnp.bfloat16

D_MODEL = 1024
GROUP_WIDTH = 256
N_GROUPS = 5
POOL_HALF_WINDOWS = (1, 2, 4, 8)
POOL_CH = 64
CONV_WIDTH = 31
SHORT_CONV_WIDTH = 3
DIFF_HEADS = 4
DIFF_HEAD_DIM = 32
MEM_HEADS = 4
MEM_HEAD_DIM = 64
PEER_HEADS = 8
PEER_N_KEYS = 128
PEER_TOPK = 16
EPS = 1e-6

LANES = 128
HALO = 16
VMEM_LIMIT = 56 * 1024 * 1024

TM_IN = 512
TQ_ATT = 256
TS_MIX = 512
TM_RET = 256
TM_EXP = 1024
A_BLOCK = 8
NEG = -1e30
LOG2E = 1.4426950408889634


def _params(sem):
    return pltpu.CompilerParams(dimension_semantics=sem, vmem_limit_bytes=VMEM_LIMIT)


def _rms(x, g):
    return x * lax.rsqrt(jnp.mean(x * x, axis=-1, keepdims=True) + EPS) * g


def _seg_rms(x, g, seg_mean):
    ms = jnp.dot(x * x, seg_mean, precision=lax.Precision.HIGHEST, preferred_element_type=F32)
    return x * lax.rsqrt(ms + EPS) * g


def _lane_range_mask(shape, lo, hi):
    lane = lax.broadcasted_iota(jnp.int32, shape, len(shape) - 1)
    return (lane >= lo) & (lane < hi)


def _inproj_kernel(x_ref, g_ref, w_ref, gq_ref, gk_ref, seg_ref,
                   upool_ref, uconv_ref, usc_ref, umem_ref, q_ref, kt_ref, v_ref):
    xb = _rms(x_ref[...], g_ref[...]).astype(BF16)

    def proj(lo, hi):
        return jnp.dot(xb, w_ref[:, lo:hi], preferred_element_type=F32)

    gw = GROUP_WIDTH
    upool_ref[...] = proj(0, gw)
    uconv_ref[...] = proj(gw, 3 * gw)
    q = proj(3 * gw, 4 * gw)
    k = proj(4 * gw, 5 * gw)
    v_ref[...] = proj(5 * gw, 6 * gw).astype(BF16)
    usc_ref[...] = proj(6 * gw, 9 * gw)
    umem_ref[...] = proj(9 * gw, 10 * gw)
    seg = seg_ref[...]
    q_ref[...] = (_seg_rms(q, gq_ref[...], seg) * (DIFF_HEAD_DIM ** -0.5 * LOG2E)).astype(BF16)
    kt_ref[0] = _seg_rms(k, gk_ref[...], seg).T.astype(BF16)


def _inproj(x2d, g, w, gq, gk, seg32, batch, seq):
    t = x2d.shape[0]
    tm = min(TM_IN, seq)
    nst = seq // tm
    gw = GROUP_WIDTH
    row = lambda c: pl.BlockSpec((tm, c), lambda i: (i, 0))
    full = lambda a: pl.BlockSpec(a.shape, lambda i: (0,) * a.ndim)
    return pl.pallas_call(
        _inproj_kernel,
        grid=(t // tm,),
        in_specs=[row(D_MODEL), full(g), full(w), full(gq), full(gk), full(seg32)],
        out_specs=[row(gw), row(2 * gw), row(3 * gw), row(gw), row(gw),
                   pl.BlockSpec((1, gw, tm), lambda i: (i // nst, 0, i % nst)), row(gw)],
        out_shape=[jax.ShapeDtypeStruct((t, gw), F32), jax.ShapeDtypeStruct((t, 2 * gw), F32),
                   jax.ShapeDtypeStruct((t, 3 * gw), F32), jax.ShapeDtypeStruct((t, gw), F32),
                   jax.ShapeDtypeStruct((t, gw), BF16), jax.ShapeDtypeStruct((batch, gw, seq), BF16),
                   jax.ShapeDtypeStruct((t, gw), BF16)],
        compiler_params=_params(("parallel",)),
        name="in_projection",
    )(x2d, g, w, gq, gk, seg32)


def _memkv_kernel(mem_ref, g_ref, w_ref, gk_ref, seg_ref, kt_ref, v_ref):
    mn = _rms(mem_ref[0], g_ref[...]).astype(BF16)
    kv = jnp.dot(mn, w_ref[...], preferred_element_type=F32)
    k = _seg_rms(kv[:, :GROUP_WIDTH], gk_ref[...], seg_ref[...])
    kt_ref[0] = k.T.astype(BF16)
    v_ref[0] = kv[:, GROUP_WIDTH:].astype(BF16)


def _memkv(mem, g, w, gk, seg64):
    b, m, _ = mem.shape
    gw = GROUP_WIDTH
    full = lambda a: pl.BlockSpec(a.shape, lambda i: (0,) * a.ndim)
    return pl.pallas_call(
        _memkv_kernel,
        grid=(b,),
        in_specs=[pl.BlockSpec((1, m, D_MODEL), lambda i: (i, 0, 0)), full(g), full(w), full(gk), full(seg64)],
        out_specs=[pl.BlockSpec((1, gw, m), lambda i: (i, 0, 0)), pl.BlockSpec((1, m, gw), lambda i: (i, 0, 0))],
        out_shape=[jax.ShapeDtypeStruct((b, gw, m), BF16), jax.ShapeDtypeStruct((b, m, gw), BF16)],
        compiler_params=_params(("parallel",)),
        name="memory_kv",
    )(mem, g, w, gk, seg64)


def _diffattn_kernel(q_ref, kt_ref, v_ref, lam_ref, laminit_ref, gsub_ref, seg_ref, o_ref):
    tq = q_ref.shape[0]
    seq = kt_ref.shape[2]
    lp = lam_ref[...]
    lam_init = laminit_ref[...]
    lam = (jnp.exp(jnp.sum(lp[0:1] * lp[1:2], axis=-1, keepdims=True))
           - jnp.exp(jnp.sum(lp[2:3] * lp[3:4], axis=-1, keepdims=True)) + lam_init)
    qpos = pl.program_id(1) * tq + lax.broadcasted_iota(jnp.int32, (tq, seq), 0)
    kpos = lax.broadcasted_iota(jnp.int32, (tq, seq), 1)
    dist = jnp.abs(qpos - kpos).astype(F32)
    q = q_ref[...]
    kt = kt_ref[0]
    v = v_ref[...]
    o = jnp.zeros((tq, GROUP_WIDTH), F32)
    half = DIFF_HEADS * DIFF_HEAD_DIM
    for h in range(DIFF_HEADS):
        slope = 2.0 ** (-8.0 * (h + 1) / DIFF_HEADS)
        bias = (slope * LOG2E) * dist
        outs = []
        for j in range(2):
            lo = j * half + h * DIFF_HEAD_DIM
            qm = jnp.where(_lane_range_mask(q.shape, lo, lo + DIFF_HEAD_DIM), q, jnp.zeros_like(q))
            s = jnp.dot(qm, kt, preferred_element_type=F32) - bias
            e = jnp.exp2(s - jnp.max(s, axis=-1, keepdims=True))
            inv_z = 1.0 / jnp.sum(e, axis=-1, keepdims=True)
            outs.append(jnp.dot(e.astype(BF16), v, preferred_element_type=F32) * inv_z)
        oh = outs[0] - lam * outs[1]
        vlo = h * 2 * DIFF_HEAD_DIM
        o = jnp.where(_lane_range_mask(o.shape, vlo, vlo + 2 * DIFF_HEAD_DIM), oh, o)
    o_ref[...] = _seg_rms(o, gsub_ref[...], seg_ref[...]) * (1.0 - lam_init)


def _diffattn(q, kt, v, lam_p, lam_init, gsub, seg64, batch, seq):
    t = q.shape[0]
    gw = GROUP_WIDTH
    tq = min(TQ_ATT, seq)
    nq = seq // tq
    full = lambda a: pl.BlockSpec(a.shape, lambda b, i: (0,) * a.ndim)
    return pl.pallas_call(
        _diffattn_kernel,
        grid=(batch, nq),
        in_specs=[pl.BlockSpec((tq, gw), lambda b, i: (b * nq + i, 0)),
                  pl.BlockSpec((1, gw, seq), lambda b, i: (b, 0, 0)),
                  pl.BlockSpec((seq, gw), lambda b, i: (b, 0)),
                  full(lam_p), full(lam_init), full(gsub), full(seg64)],
        out_specs=pl.BlockSpec((tq, gw), lambda b, i: (b * nq + i, 0)),
        out_shape=jax.ShapeDtypeStruct((t, gw), F32),
        compiler_params=_params(("parallel", "parallel")),
        name="diff_attention",
    )(q, kt, v, lam_p, lam_init, gsub, seg64)


def _mixer_kernel(x_ref, up_ref, up_prev, up_next, uc_ref, uc_prev, uc_next, us_ref, us_prev, us_next,
                  um_ref, yd_ref, kmt_ref, vm_ref,
                  poolw_ref, pools_ref, dw_ref, lng_ref, lnb_ref, pw_ref, sw_ref, gqm_ref, seg_ref,
                  gn_ref, wout_ref, o_ref, ext_pool, ext_conv, ext_sc, *, seq):
    ts = x_ref.shape[0]
    gw = GROUP_WIDTH
    nst = seq // ts
    it = pl.program_id(0) % nst
    keep_prev = jnp.where(it > 0, 1.0, 0.0)
    keep_next = jnp.where(it < nst - 1, 1.0, 0.0)

    def fill(ext, prev, main, nxt):
        ext[0:HALO, :] = prev * keep_prev
        ext[HALO:HALO + ts, :] = main
        ext[HALO + ts:HALO + ts + HALO, :] = nxt * keep_next

    def shifted(ext, k):
        return ext[HALO + k:HALO + k + ts, :]

    fill(ext_pool, up_prev[...], up_ref[...], up_next[...])
    u = up_ref[...]
    lane = lax.broadcasted_iota(jnp.int32, (ts, gw), 1)
    pos = it * ts + lax.broadcasted_iota(jnp.int32, (ts, gw), 0)
    wsum = jnp.zeros((ts, gw), F32)
    halfw = jnp.zeros((ts, gw), jnp.int32)
    run = None
    prev_half = 0
    for gi, hw in enumerate(POOL_HALF_WINDOWS):
        for k in list(range(-hw, -prev_half)) + list(range(prev_half, hw)):
            term = shifted(ext_pool, k)
            run = term if run is None else run + term
        prev_half = hw
        in_group = (lane >= gi * POOL_CH) & (lane < (gi + 1) * POOL_CH)
        wsum = jnp.where(in_group, run, wsum)
        halfw = jnp.where(in_group, hw, halfw)
    cnt = (jnp.minimum(pos + halfw, seq) - jnp.maximum(pos - halfw, 0)).astype(F32)
    pooled = wsum / cnt - u
    y_pool = jnp.dot(pooled.astype(BF16), poolw_ref[...], preferred_element_type=F32) * pools_ref[...]

    def glu(ucv):
        return ucv[:, :gw] * jax.nn.sigmoid(ucv[:, gw:])

    fill(ext_conv, glu(uc_prev[...]), glu(uc_ref[...]), glu(uc_next[...]))
    pad = CONV_WIDTH // 2
    conv = jnp.zeros((ts, gw), F32)
    for k in range(CONV_WIDTH):
        conv = conv + shifted(ext_conv, k - pad) * dw_ref[k:k + 1, :]
    mu = jnp.mean(conv, axis=-1, keepdims=True)
    cen = conv - mu
    var = jnp.mean(cen * cen, axis=-1, keepdims=True)
    hln = cen * lax.rsqrt(var + EPS) * lng_ref[...] + lnb_ref[...]
    hact = hln * jax.nn.sigmoid(hln)
    y_conv = jnp.dot(hact.astype(BF16), pw_ref[...], preferred_element_type=F32)

    def ch(usv):
        return usv[:, 2 * gw:] * usv[:, :gw]

    fill(ext_sc, ch(us_prev[...]), ch(us_ref[...]), ch(us_next[...]))
    spad = SHORT_CONV_WIDTH // 2
    sconv = jnp.zeros((ts, gw), F32)
    for k in range(SHORT_CONV_WIDTH):
        sconv = sconv + shifted(ext_sc, k - spad) * sw_ref[k:k + 1, :]
    y_sc = us_ref[:, gw:2 * gw] * sconv

    qn = (_seg_rms(um_ref[...], gqm_ref[...], seg_ref[...]) * (MEM_HEAD_DIM ** -0.5)).astype(BF16)
    kmt = kmt_ref[0]
    vm = vm_ref[0]
    y_mem = jnp.zeros((ts, gw), F32)
    for h in range(MEM_HEADS):
        hmask = _lane_range_mask((ts, gw), h * MEM_HEAD_DIM, (h + 1) * MEM_HEAD_DIM)
        qm = jnp.where(hmask, qn, jnp.zeros_like(qn))
        s = jnp.dot(qm, kmt, preferred_element_type=F32)
        e = jnp.exp(s - jnp.max(s, axis=-1, keepdims=True))
        p = (e * (1.0 / jnp.sum(e, axis=-1, keepdims=True))).astype(BF16)
        y_mem = jnp.where(hmask, jnp.dot(p, vm, preferred_element_type=F32), y_mem)

    acc = x_ref[...]
    for gi, y in enumerate((y_pool, y_conv, yd_ref[...], y_sc, y_mem)):
        yn = _rms(y, gn_ref[gi:gi + 1, :]).astype(BF16)
        acc = acc + jnp.dot(yn, wout_ref[gi * gw:(gi + 1) * gw, :], preferred_element_type=F32)
    o_ref[...] = acc


def _mixers(x2d, upool, uconv, usc, umem, ydiff, kmt, vm, poolw, pools, dw, lng, lnb, pw, sw, gqm, seg64,
            gn, wout, batch, seq):
    t = x2d.shape[0]
    gw = GROUP_WIDTH
    ts = min(TS_MIX, seq)
    nst = seq // ts
    r = ts // HALO
    last = t // HALO - 1
    row = lambda c: pl.BlockSpec((ts, c), lambda i: (i, 0))
    prev = lambda c: pl.BlockSpec((HALO, c), lambda i: (jnp.maximum(i * r - 1, 0), 0))
    nxt = lambda c: pl.BlockSpec((HALO, c), lambda i: (jnp.minimum((i + 1) * r, last), 0))
    full = lambda a: pl.BlockSpec(a.shape, lambda i: (0,) * a.ndim)
    perb = lambda a: pl.BlockSpec((1,) + a.shape[1:], lambda i: (i // nst, 0, 0))
    consts = (poolw, pools, dw, lng, lnb, pw, sw, gqm, seg64, gn, wout)
    return pl.pallas_call(
        functools.partial(_mixer_kernel, seq=seq),
        grid=(t // ts,),
        in_specs=[row(D_MODEL),
                  row(gw), prev(gw), nxt(gw),
                  row(2 * gw), prev(2 * gw), nxt(2 * gw),
                  row(3 * gw), prev(3 * gw), nxt(3 * gw),
                  row(gw), row(gw), perb(kmt), perb(vm)] + [full(a) for a in consts],
        out_specs=row(D_MODEL),
        out_shape=jax.ShapeDtypeStruct((t, D_MODEL), F32),
        scratch_shapes=[pltpu.VMEM((ts + 2 * HALO, gw), F32)] * 3,
        compiler_params=_params(("parallel",)),
        name="mixers_out_projection",
    )(x2d, upool, upool, upool, uconv, uconv, uconv, usc, usc, usc, umem, ydiff, kmt, vm, *consts)


def _sorting_network(n):
    pairs = []
    p = 1
    while p < n:
        k = p
        while k >= 1:
            for j in range(k % p, n - k, 2 * k):
                for i in range(min(k, n - j - k)):
                    if (i + j) // (2 * p) == (i + j + k) // (2 * p):
                        pairs.append((i + j, i + j + k))
            k //= 2
        p *= 2
    return pairs


def _sort_blocks_descending(blocks):
    blocks = list(blocks)
    for i, j in _sorting_network(len(blocks)):
        hi, lo = jnp.maximum(blocks[i], blocks[j]), jnp.minimum(blocks[i], blocks[j])
        blocks[i], blocks[j] = hi, lo
    return blocks


def _pop_top_values(stack, singles, n):
    stack, singles = list(stack), list(singles)
    vals = []
    for k in range(n):
        top = stack[0]
        for blk in singles:
            top = jnp.maximum(top, blk)
        m = jnp.max(top, axis=0, keepdims=True)
        vals.append(m)
        keep = min(len(stack), n - k - 1)
        if keep == 0:
            break
        hit = stack[0] == m
        stack = [jnp.where(hit, stack[r + 1] if r + 1 < len(stack) else NEG, stack[r]) for r in range(keep)]
        singles = [jnp.where(blk == m, NEG, blk) for blk in singles]
    return vals


def _retrieval_kernel(x_ref, g_ref, wqt_ref, sk_ref, xnt_ref, thr_ref, e1_ref, e2_ref):
    tm = x_ref.shape[0]
    nk = PEER_N_KEYS
    n = PEER_TOPK + 1
    xb = _rms(x_ref[...], g_ref[...]).T.astype(BF16)
    xnt_ref[...] = xb
    qt = jnp.dot(wqt_ref[...], xb, preferred_element_type=F32)
    row_id = lax.broadcasted_iota(jnp.int32, (8, tm), 0)
    for h in range(PEER_HEADS):
        sc = []
        for j in range(2):
            r0 = (2 * h + j) * nk
            sc.append(jnp.dot(sk_ref[2 * h + j], qt[r0:r0 + nk].astype(BF16), preferred_element_type=F32))
        v1, v2 = (_pop_top_values(_sort_blocks_descending([s[r:r + 8] for r in range(0, nk, 8)]), [], n)
                  for s in sc)
        v2_blocks = []
        for r0 in range(0, n, 8):
            blk = jnp.full((8, tm), NEG, F32)
            for i in range(r0, min(r0 + 8, n)):
                blk = jnp.where(row_id == i - r0, v2[i], blk)
            v2_blocks.append(blk)
        stack = [v1[0] + v2_blocks[0]]
        for i in range(1, n):
            stack.append(v1[i] + jnp.where(row_id < n // (i + 1), v2_blocks[0], NEG))
        best = _pop_top_values(stack, [v1[0] + blk for blk in v2_blocks[1:]], n)
        z = jnp.zeros((1, tm), F32)
        for i in range(PEER_TOPK):
            z = z + jnp.exp(best[i] - best[0])
        half_inv_z = 0.5 / z
        tau = 0.5 * (best[PEER_TOPK - 1] + best[PEER_TOPK])
        thr = jnp.exp(tau - sc[0] - v2[0]) * half_inv_z
        e1 = jnp.exp(sc[0] - v1[0])
        e2 = jnp.exp(sc[1] - v2[0]) * half_inv_z
        for c in range(tm // LANES):
            cs = slice(c * LANES, (c + 1) * LANES)
            thr_ref[h, c] = thr[:, cs]
            e1_ref[h, c] = e1[:, cs]
            e2_ref[h, c] = e2[:, cs]


def _retrieval(x2d, g, wqt, sk):
    t = x2d.shape[0]
    tm = min(TM_RET, t)
    nk = PEER_N_KEYS
    full = lambda a: pl.BlockSpec(a.shape, lambda i: (0,) * a.ndim)
    tab = pl.BlockSpec((PEER_HEADS, tm // LANES, nk, LANES), lambda i: (0, i, 0, 0))
    tab_shape = jax.ShapeDtypeStruct((PEER_HEADS, t // LANES, nk, LANES), F32)
    return pl.pallas_call(
        _retrieval_kernel,
        grid=(t // tm,),
        in_specs=[pl.BlockSpec((tm, D_MODEL), lambda i: (i, 0)), full(g), full(wqt), full(sk)],
        out_specs=[pl.BlockSpec((D_MODEL, tm), lambda i: (0, i)), tab, tab, tab],
        out_shape=[jax.ShapeDtypeStruct((D_MODEL, t), BF16), tab_shape, tab_shape, tab_shape],
        compiler_params=_params(("parallel",)),
        name="peer_retrieval",
    )(x2d, g, wqt, sk)


def _expert_kernel(x_ref, xnt_ref, u_ref, vt_ref, thr_ref, e1_ref, e2_ref, o_ref, acc_ref, w_ref):
    nk = PEER_N_KEYS
    eb = pl.program_id(1)

    @pl.when(eb == 0)
    def _():
        acc_ref[...] = jnp.zeros_like(acc_ref)

    tm = xnt_ref.shape[1]
    ht = jnp.dot(u_ref[...], xnt_ref[...], preferred_element_type=F32)
    for al in range(A_BLOCK):
        for c in range(tm // LANES):
            cs = slice(c * LANES, (c + 1) * LANES)
            hs = ht[al * nk:(al + 1) * nk, cs]
            gate = None
            for h in range(PEER_HEADS):
                e2 = e2_ref[h, c]
                sel = jnp.where(e2 >= thr_ref[h, c, al:al + 1, :], e2, 0.0)
                term = e1_ref[h, c, al:al + 1, :] * sel
                gate = term if gate is None else gate + term
            w_ref[al * nk:(al + 1) * nk, cs] = (gate * (hs * (1.0 + lax.erf(hs * math.sqrt(0.5))))).astype(BF16)
    acc_ref[...] += jnp.dot(vt_ref[0], w_ref[...], preferred_element_type=F32)

    @pl.when(eb == pl.num_programs(1) - 1)
    def _():
        o_ref[...] = x_ref[...] + acc_ref[...].T


def _experts(x2d, xnt, u, vt, thr, e1, e2):
    t = x2d.shape[0]
    tm = min(TM_EXP, t)
    nk = PEER_N_KEYS
    te = A_BLOCK * nk
    nc = tm // LANES
    tab_a = pl.BlockSpec((PEER_HEADS, nc, A_BLOCK, LANES), lambda i, e: (0, i, e, 0))
    tab_b = pl.BlockSpec((PEER_HEADS, nc, nk, LANES), lambda i, e: (0, i, 0, 0))
    return pl.pallas_call(
        _expert_kernel,
        grid=(t // tm, nk // A_BLOCK),
        in_specs=[pl.BlockSpec((tm, D_MODEL), lambda i, e: (i, 0)),
                  pl.BlockSpec((D_MODEL, tm), lambda i, e: (0, i)),
                  pl.BlockSpec((te, D_MODEL), lambda i, e: (e, 0)),
                  pl.BlockSpec((1, D_MODEL, te), lambda i, e: (e, 0, 0)),
                  tab_a, tab_a, tab_b],
        out_specs=pl.BlockSpec((tm, D_MODEL), lambda i, e: (i, 0)),
        out_shape=jax.ShapeDtypeStruct((t, D_MODEL), F32),
        scratch_shapes=[pltpu.VMEM((D_MODEL, tm), F32), pltpu.VMEM((te, tm), BF16)],
        compiler_params=_params(("parallel", "arbitrary")),
        name="peer_experts",
    )(x2d, xnt, u, vt, thr, e1, e2)


def _segment_mean_matrix(width, seg):
    idx = jnp.arange(width) // seg
    return (idx[:, None] == idx[None, :]).astype(F32) / seg


def _block_diag(w):
    g, c, _ = w.shape
    eye = jnp.eye(g, dtype=w.dtype)
    return (eye[:, None, :, None] * w[:, :, None, :]).reshape(g * c, g * c)


def kernel(x, mem, norm_mix, w_in, pool_w, pool_scale, conv_dw, conv_ln_g, conv_ln_b, conv_pw, diff_qk_norm,
           diff_lambda, diff_subln, sconv_w, mem_norm, w_mem_kv, mem_qk_norm, group_norm, w_out, norm_ffn,
           peer_wq, peer_subkeys, peer_u, peer_v):
    b, s, d = x.shape
    depth = w_in.shape[0]
    gw = GROUP_WIDTH
    seg32 = _segment_mean_matrix(gw, DIFF_HEAD_DIM)
    seg64 = _segment_mean_matrix(gw, MEM_HEAD_DIM)
    row = lambda a: a.reshape(1, -1)
    tile = lambda a, n: jnp.tile(a, n).reshape(1, -1)
    x2d = x.reshape(b * s, d)
    for l in range(depth):
        lam_init = jnp.full((1, 1), 0.8 - 0.6 * math.exp(-0.3 * l), F32)
        upool, uconv, usc, umem, q, kt, v = _inproj(
            x2d, row(norm_mix[l]), w_in[l].astype(BF16), tile(diff_qk_norm[l, 0], 2 * DIFF_HEADS),
            tile(diff_qk_norm[l, 1], 2 * DIFF_HEADS), seg32, b, s)
        kmt, vm = _memkv(mem, row(mem_norm[l]), w_mem_kv[l].astype(BF16), tile(mem_qk_norm[l, 1], MEM_HEADS), seg64)
        ydiff = _diffattn(q, kt, v, diff_lambda[l], lam_init, tile(diff_subln[l], DIFF_HEADS), seg64, b, s)
        x2d = _mixers(x2d, upool, uconv, usc, umem, ydiff, kmt, vm,
                      _block_diag(pool_w[l]).astype(BF16), row(pool_scale[l]), conv_dw[l], row(conv_ln_g[l]),
                      row(conv_ln_b[l]), conv_pw[l].astype(BF16), sconv_w[l], tile(mem_qk_norm[l, 0], MEM_HEADS),
                      seg64, group_norm[l].reshape(N_GROUPS, gw), w_out[l].astype(BF16), b, s)
        xnt, thr, e1, e2 = _retrieval(
            x2d, row(norm_ffn[l]), peer_wq[l].T.astype(BF16),
            peer_subkeys[l].reshape(2 * PEER_HEADS, PEER_N_KEYS, -1).astype(BF16))
        vt = peer_v[l].astype(BF16).reshape(-1, A_BLOCK * PEER_N_KEYS, d).transpose(0, 2, 1)
        x2d = _experts(x2d, xnt, peer_u[l].astype(BF16), vt, thr, e1, e2)
    return x2d.reshape(b, s, d)
```

```python
import functools
import math

import jax
import jax.numpy as jnp
from jax import lax
from jax.experimental import pallas as pl
from jax.experimental.pallas import tpu as pltpu

F32 = jnp.float32
BF16 = jnp.bfloat16

D_MODEL = 1024
GROUP_WIDTH = 256
N_GROUPS = 5
POOL_HALF_WINDOWS = (1, 2, 4, 8)
POOL_CH = 64
CONV_WIDTH = 31
SHORT_CONV_WIDTH = 3
DIFF_HEADS = 4
DIFF_HEAD_DIM = 32
MEM_HEADS = 4
MEM_HEAD_DIM = 64
PEER_HEADS = 8
PEER_N_KEYS = 128
PEER_TOPK = 16
EPS = 1e-6

LANES = 128
SUBLANES = 8
HALO = 16
VMEM_LIMIT = 56 * 1024 * 1024

TM_IN = 512
TQ_ATT = 256
TS_MIX = 512
TM_RET = 256
TM_EXP = 1024
A_BLOCK = 8
NEG = -1e30
LOG2E = 1.4426950408889634


def _params(sem):
    return pltpu.CompilerParams(dimension_semantics=sem, vmem_limit_bytes=VMEM_LIMIT)


def _rms(x, g):
    return x * lax.rsqrt(jnp.mean(x * x, axis=-1, keepdims=True) + EPS) * g


def _seg_rms(x, g, seg_mean):
    ms = jnp.dot(x * x, seg_mean, precision=lax.Precision.HIGHEST, preferred_element_type=F32)
    return x * lax.rsqrt(ms + EPS) * g


def _lane_range_mask(shape, lo, hi):
    lane = lax.broadcasted_iota(jnp.int32, shape, len(shape) - 1)
    return (lane >= lo) & (lane < hi)


def _inproj_kernel(x_ref, g_ref, w_ref, gq_ref, gk_ref, seg_ref,
                   upool_ref, uconv_ref, usc_ref, umem_ref, q_ref, kt_ref, v_ref):
    xb = _rms(x_ref[...], g_ref[...]).astype(BF16)

    def proj(lo, hi):
        return jnp.dot(xb, w_ref[:, lo:hi], preferred_element_type=F32)

    gw = GROUP_WIDTH
    upool_ref[...] = proj(0, gw)
    uconv_ref[...] = proj(gw, 3 * gw)
    q = proj(3 * gw, 4 * gw)
    k = proj(4 * gw, 5 * gw)
    v_ref[...] = proj(5 * gw, 6 * gw).astype(BF16)
    usc_ref[...] = proj(6 * gw, 9 * gw)
    umem_ref[...] = proj(9 * gw, 10 * gw)
    seg = seg_ref[...]
    q_ref[...] = (_seg_rms(q, gq_ref[...], seg) * (DIFF_HEAD_DIM ** -0.5 * LOG2E)).astype(BF16)
    kt_ref[0] = _seg_rms(k, gk_ref[...], seg).T.astype(BF16)


def _inproj(x2d, g, w, gq, gk, seg32, batch, seq):
    t = x2d.shape[0]
    tm = min(TM_IN, seq)
    nst = seq // tm
    gw = GROUP_WIDTH
    row = lambda c: pl.BlockSpec((tm, c), lambda i: (i, 0))
    full = lambda a: pl.BlockSpec(a.shape, lambda i: (0,) * a.ndim)
    return pl.pallas_call(
        _inproj_kernel,
        grid=(t // tm,),
        in_specs=[row(D_MODEL), full(g), full(w), full(gq), full(gk), full(seg32)],
        out_specs=[row(gw), row(2 * gw), row(3 * gw), row(gw), row(gw),
                   pl.BlockSpec((1, gw, tm), lambda i: (i // nst, 0, i % nst)), row(gw)],
        out_shape=[jax.ShapeDtypeStruct((t, gw), F32), jax.ShapeDtypeStruct((t, 2 * gw), F32),
                   jax.ShapeDtypeStruct((t, 3 * gw), F32), jax.ShapeDtypeStruct((t, gw), F32),
                   jax.ShapeDtypeStruct((t, gw), BF16), jax.ShapeDtypeStruct((batch, gw, seq), BF16),
                   jax.ShapeDtypeStruct((t, gw), BF16)],
        compiler_params=_params(("parallel",)),
        name="in_projection",
    )(x2d, g, w, gq, gk, seg32)


def _memkv_kernel(mem_ref, g_ref, w_ref, gk_ref, seg_ref, kt_ref, v_ref):
    mn = _rms(mem_ref[0], g_ref[...]).astype(BF16)
    kv = jnp.dot(mn, w_ref[...], preferred_element_type=F32)
    k = _seg_rms(kv[:, :GROUP_WIDTH], gk_ref[...], seg_ref[...])
    kt_ref[0] = k.T.astype(BF16)
    v_ref[0] = kv[:, GROUP_WIDTH:].astype(BF16)


def _memkv(mem, g, w, gk, seg64):
    b, m, _ = mem.shape
    gw = GROUP_WIDTH
    full = lambda a: pl.BlockSpec(a.shape, lambda i: (0,) * a.ndim)
    return pl.pallas_call(
        _memkv_kernel,
        grid=(b,),
        in_specs=[pl.BlockSpec((1, m, D_MODEL), lambda i: (i, 0, 0)), full(g), full(w), full(gk), full(seg64)],
        out_specs=[pl.BlockSpec((1, gw, m), lambda i: (i, 0, 0)), pl.BlockSpec((1, m, gw), lambda i: (i, 0, 0))],
        out_shape=[jax.ShapeDtypeStruct((b, gw, m), BF16), jax.ShapeDtypeStruct((b, m, gw), BF16)],
        compiler_params=_params(("parallel",)),
        name="memory_kv",
    )(mem, g, w, gk, seg64)


def _diffattn_kernel(q_ref, kt_ref, v_ref, lam_ref, laminit_ref, gsub_ref, seg_ref, o_ref):
    tq = q_ref.shape[0]
    seq = kt_ref.shape[2]
    lp = lam_ref[...]
    lam_init = laminit_ref[...]
    lam = (jnp.exp(jnp.sum(lp[0:1] * lp[1:2], axis=-1, keepdims=True))
           - jnp.exp(jnp.sum(lp[2:3] * lp[3:4], axis=-1, keepdims=True)) + lam_init)
    qpos = pl.program_id(1) * tq + lax.broadcasted_iota(jnp.int32, (tq, seq), 0)
    kpos = lax.broadcasted_iota(jnp.int32, (tq, seq), 1)
    dist = jnp.abs(qpos - kpos).astype(F32)
    q = q_ref[...]
    kt = kt_ref[0]
    v = v_ref[...]
    o = jnp.zeros((tq, GROUP_WIDTH), F32)
    half = DIFF_HEADS * DIFF_HEAD_DIM
    for h in range(DIFF_HEADS):
        slope = 2.0 ** (-8.0 * (h + 1) / DIFF_HEADS)
        bias = (slope * LOG2E) * dist
        vlo = h * 2 * DIFF_HEAD_DIM
        zlane = (vlo + 2 * DIFF_HEAD_DIM) % GROUP_WIDTH
        v_ones = jnp.where(_lane_range_mask(v.shape, zlane, zlane + 1), jnp.ones_like(v), v)
        outs = []
        for j in range(2):
            lo = j * half + h * DIFF_HEAD_DIM
            qm = jnp.where(_lane_range_mask(q.shape, lo, lo + DIFF_HEAD_DIM), q, jnp.zeros_like(q))
            s = jnp.dot(qm, kt, preferred_element_type=F32) - bias
            e = jnp.exp2(s - jnp.max(s, axis=-1, keepdims=True))
            ev = jnp.dot(e.astype(BF16), v_ones, preferred_element_type=F32)
            outs.append(ev * (1.0 / ev[:, zlane:zlane + 1]))
        oh = outs[0] - lam * outs[1]
        o = jnp.where(_lane_range_mask(o.shape, vlo, vlo + 2 * DIFF_HEAD_DIM), oh, o)
    o_ref[...] = _seg_rms(o, gsub_ref[...], seg_ref[...]) * (1.0 - lam_init)


def _diffattn(q, kt, v, lam_p, lam_init, gsub, seg64, batch, seq):
    t = q.shape[0]
    gw = GROUP_WIDTH
    tq = min(TQ_ATT, seq)
    nq = seq // tq
    full = lambda a: pl.BlockSpec(a.shape, lambda b, i: (0,) * a.ndim)
    return pl.pallas_call(
        _diffattn_kernel,
        grid=(batch, nq),
        in_specs=[pl.BlockSpec((tq, gw), lambda b, i: (b * nq + i, 0)),
                  pl.BlockSpec((1, gw, seq), lambda b, i: (b, 0, 0)),
                  pl.BlockSpec((seq, gw), lambda b, i: (b, 0)),
                  full(lam_p), full(lam_init), full(gsub), full(seg64)],
        out_specs=pl.BlockSpec((tq, gw), lambda b, i: (b * nq + i, 0)),
        out_shape=jax.ShapeDtypeStruct((t, gw), F32),
        compiler_params=_params(("parallel", "parallel")),
        name="diff_attention",
    )(q, kt, v, lam_p, lam_init, gsub, seg64)


def _mixer_kernel(x_ref, up_ref, up_prev, up_next, uc_ref, uc_prev, uc_next, us_ref, us_prev, us_next,
                  um_ref, yd_ref, kmt_ref, vm_ref,
                  poolw_ref, pools_ref, dw_ref, lng_ref, lnb_ref, pw_ref, sw_ref, gqm_ref, seg_ref,
                  gn_ref, wout_ref, o_ref, ext_pool, ext_conv, ext_sc, shift_conv, *, seq):
    ts = x_ref.shape[0]
    gw = GROUP_WIDTH
    nst = seq // ts
    it = pl.program_id(0) % nst
    keep_prev = jnp.where(it > 0, 1.0, 0.0)
    keep_next = jnp.where(it < nst - 1, 1.0, 0.0)

    def fill(ext, prev, main, nxt):
        ext[0:HALO, :] = prev * keep_prev
        ext[HALO:HALO + ts, :] = main
        ext[HALO + ts:HALO + ts + HALO, :] = nxt * keep_next

    def shifted(ext, k):
        return ext[HALO + k:HALO + k + ts, :]

    fill(ext_pool, up_prev[...], up_ref[...], up_next[...])
    u = up_ref[...]
    lane = lax.broadcasted_iota(jnp.int32, (ts, gw), 1)
    pos = it * ts + lax.broadcasted_iota(jnp.int32, (ts, gw), 0)
    wsum = jnp.zeros((ts, gw), F32)
    halfw = jnp.zeros((ts, gw), jnp.int32)
    run = None
    prev_half = 0
    for gi, hw in enumerate(POOL_HALF_WINDOWS):
        for k in list(range(-hw, -prev_half)) + list(range(prev_half, hw)):
            term = shifted(ext_pool, k)
            run = term if run is None else run + term
        prev_half = hw
        in_group = (lane >= gi * POOL_CH) & (lane < (gi + 1) * POOL_CH)
        wsum = jnp.where(in_group, run, wsum)
        halfw = jnp.where(in_group, hw, halfw)
    cnt = (jnp.minimum(pos + halfw, seq) - jnp.maximum(pos - halfw, 0)).astype(F32)
    pooled = wsum / cnt - u
    y_pool = jnp.dot(pooled.astype(BF16), poolw_ref[...], preferred_element_type=F32) * pools_ref[...]

    def glu(ucv):
        return ucv[:, :gw] * jax.nn.sigmoid(ucv[:, gw:])

    fill(ext_conv, glu(uc_prev[...]), glu(uc_ref[...]), glu(uc_next[...]))
    pad = CONV_WIDTH // 2
    for r in range(SUBLANES):
        shift_conv[r] = ext_conv[r:r + shift_conv.shape[1], :]
    conv = jnp.zeros((ts, gw), F32)
    for k in range(CONV_WIDTH):
        m, r = divmod(HALO - pad + k, SUBLANES)
        conv = conv + shift_conv[r, SUBLANES * m:SUBLANES * m + ts, :] * dw_ref[k:k + 1, :]
    mu = jnp.mean(conv, axis=-1, keepdims=True)
    cen = conv - mu
    var = jnp.mean(cen * cen, axis=-1, keepdims=True)
    hln = cen * lax.rsqrt(var + EPS) * lng_ref[...] + lnb_ref[...]
    hact = hln * jax.nn.sigmoid(hln)
    y_conv = jnp.dot(hact.astype(BF16), pw_ref[...], preferred_element_type=F32)

    def ch(usv):
        return usv[:, 2 * gw:] * usv[:, :gw]

    fill(ext_sc, ch(us_prev[...]), ch(us_ref[...]), ch(us_next[...]))
    spad = SHORT_CONV_WIDTH // 2
    sconv = jnp.zeros((ts, gw), F32)
    for k in range(SHORT_CONV_WIDTH):
        sconv = sconv + shifted(ext_sc, k - spad) * sw_ref[k:k + 1, :]
    y_sc = us_ref[:, gw:2 * gw] * sconv

    qn = (_seg_rms(um_ref[...], gqm_ref[...], seg_ref[...]) * (MEM_HEAD_DIM ** -0.5)).astype(BF16)
    kmt = kmt_ref[0]
    vm = vm_ref[0]
    y_mem = jnp.zeros((ts, gw), F32)
    for h in range(MEM_HEADS):
        hmask = _lane_range_mask((ts, gw), h * MEM_HEAD_DIM, (h + 1) * MEM_HEAD_DIM)
        qm = jnp.where(hmask, qn, jnp.zeros_like(qn))
        s = jnp.dot(qm, kmt, preferred_element_type=F32)
        e = jnp.exp(s - jnp.max(s, axis=-1, keepdims=True))
        p = (e * (1.0 / jnp.sum(e, axis=-1, keepdims=True))).astype(BF16)
        y_mem = jnp.where(hmask, jnp.dot(p, vm, preferred_element_type=F32), y_mem)

    acc = x_ref[...]
    for gi, y in enumerate((y_pool, y_conv, yd_ref[...], y_sc, y_mem)):
        yn = _rms(y, gn_ref[gi:gi + 1, :]).astype(BF16)
        acc = acc + jnp.dot(yn, wout_ref[gi * gw:(gi + 1) * gw, :], preferred_element_type=F32)
    o_ref[...] = acc


def _mixers(x2d, upool, uconv, usc, umem, ydiff, kmt, vm, poolw, pools, dw, lng, lnb, pw, sw, gqm, seg64,
            gn, wout, batch, seq):
    t = x2d.shape[0]
    gw = GROUP_WIDTH
    ts = min(TS_MIX, seq)
    nst = seq // ts
    r = ts // HALO
    last = t // HALO - 1
    row = lambda c: pl.BlockSpec((ts, c), lambda i: (i, 0))
    prev = lambda c: pl.BlockSpec((HALO, c), lambda i: (jnp.maximum(i * r - 1, 0), 0))
    nxt = lambda c: pl.BlockSpec((HALO, c), lambda i: (jnp.minimum((i + 1) * r, last), 0))
    full = lambda a: pl.BlockSpec(a.shape, lambda i: (0,) * a.ndim)
    perb = lambda a: pl.BlockSpec((1,) + a.shape[1:], lambda i: (i // nst, 0, 0))
    consts = (poolw, pools, dw, lng, lnb, pw, sw, gqm, seg64, gn, wout)
    return pl.pallas_call(
        functools.partial(_mixer_kernel, seq=seq),
        grid=(t // ts,),
        in_specs=[row(D_MODEL),
                  row(gw), prev(gw), nxt(gw),
                  row(2 * gw), prev(2 * gw), nxt(2 * gw),
                  row(3 * gw), prev(3 * gw), nxt(3 * gw),
                  row(gw), row(gw), perb(kmt), perb(vm)] + [full(a) for a in consts],
        out_specs=row(D_MODEL),
        out_shape=jax.ShapeDtypeStruct((t, D_MODEL), F32),
        scratch_shapes=[pltpu.VMEM((ts + 2 * HALO, gw), F32)] * 3
        + [pltpu.VMEM((SUBLANES, ts + 2 * HALO - SUBLANES, gw), F32)],
        compiler_params=_params(("parallel",)),
        name="mixers_out_projection",
    )(x2d, upool, upool, upool, uconv, uconv, uconv, usc, usc, usc, umem, ydiff, kmt, vm, *consts)


def _sorting_network(n):
    pairs = []
    p = 1
    while p < n:
        k = p
        while k >= 1:
            for j in range(k % p, n - k, 2 * k):
                for i in range(min(k, n - j - k)):
                    if (i + j) // (2 * p) == (i + j + k) // (2 * p):
                        pairs.append((i + j, i + j + k))
            k //= 2
        p *= 2
    return pairs


def _sort_blocks_descending(blocks):
    blocks = list(blocks)
    for i, j in _sorting_network(len(blocks)):
        hi, lo = jnp.maximum(blocks[i], blocks[j]), jnp.minimum(blocks[i], blocks[j])
        blocks[i], blocks[j] = hi, lo
    return blocks


def _pop_top_values(stack, singles, n):
    stack, singles = list(stack), list(singles)
    vals = []
    for k in range(n):
        top = stack[0]
        for blk in singles:
            top = jnp.maximum(top, blk)
        m = jnp.max(top, axis=0, keepdims=True)
        vals.append(m)
        keep = min(len(stack), n - k - 1)
        if keep == 0:
            break
        hit = stack[0] == m
        stack = [jnp.where(hit, stack[r + 1] if r + 1 < len(stack) else NEG, stack[r]) for r in range(keep)]
        singles = [jnp.where(blk == m, NEG, blk) for blk in singles]
    return vals


def _retrieval_kernel(x_ref, g_ref, wqt_ref, sk_ref, xnt_ref, thr_ref, e1_ref, e2_ref):
    tm = x_ref.shape[0]
    nk = PEER_N_KEYS
    n = PEER_TOPK + 1
    xb = _rms(x_ref[...], g_ref[...]).T.astype(BF16)
    xnt_ref[...] = xb
    qt = jnp.dot(wqt_ref[...], xb, preferred_element_type=F32)
    row_id = lax.broadcasted_iota(jnp.int32, (8, tm), 0)
    for h in range(PEER_HEADS):
        sc = []
        for j in range(2):
            r0 = (2 * h + j) * nk
            sc.append(jnp.dot(sk_ref[2 * h + j], qt[r0:r0 + nk].astype(BF16), preferred_element_type=F32))
        v1, v2 = (_pop_top_values(_sort_blocks_descending([s[r:r + 8] for r in range(0, nk, 8)]), [], n)
                  for s in sc)
        v2_blocks = []
        for r0 in range(0, n, 8):
            blk = jnp.full((8, tm), NEG, F32)
            for i in range(r0, min(r0 + 8, n)):
                blk = jnp.where(row_id == i - r0, v2[i], blk)
            v2_blocks.append(blk)
        stack = [v1[0] + v2_blocks[0]]
        for i in range(1, n):
            stack.append(v1[i] + jnp.where(row_id < n // (i + 1), v2_blocks[0], NEG))
        best = _pop_top_values(stack, [v1[0] + blk for blk in v2_blocks[1:]], n)
        z = jnp.zeros((1, tm), F32)
        for i in range(PEER_TOPK):
            z = z + jnp.exp(best[i] - best[0])
        half_inv_z = 0.5 / z
        tau = 0.5 * (best[PEER_TOPK - 1] + best[PEER_TOPK])
        thr = jnp.exp(tau - sc[0] - v2[0]) * half_inv_z
        e1 = jnp.exp(sc[0] - v1[0])
        e2 = jnp.exp(sc[1] - v2[0]) * half_inv_z
        for c in range(tm // LANES):
            cs = slice(c * LANES, (c + 1) * LANES)
            thr_ref[h, c] = thr[:, cs]
            e1_ref[h, c] = e1[:, cs]
            e2_ref[h, c] = e2[:, cs]


def _retrieval(x2d, g, wqt, sk):
    t = x2d.shape[0]
    tm = min(TM_RET, t)
    nk = PEER_N_KEYS
    full = lambda a: pl.BlockSpec(a.shape, lambda i: (0,) * a.ndim)
    tab = pl.BlockSpec((PEER_HEADS, tm // LANES, nk, LANES), lambda i: (0, i, 0, 0))
    tab_shape = jax.ShapeDtypeStruct((PEER_HEADS, t // LANES, nk, LANES), F32)
    return pl.pallas_call(
        _retrieval_kernel,
        grid=(t // tm,),
        in_specs=[pl.BlockSpec((tm, D_MODEL), lambda i: (i, 0)), full(g), full(wqt), full(sk)],
        out_specs=[pl.BlockSpec((D_MODEL, tm), lambda i: (0, i)), tab, tab, tab],
        out_shape=[jax.ShapeDtypeStruct((D_MODEL, t), BF16), tab_shape, tab_shape, tab_shape],
        compiler_params=_params(("parallel",)),
        name="peer_retrieval",
    )(x2d, g, wqt, sk)


def _expert_kernel(x_ref, xnt_ref, u_ref, vt_ref, thr_ref, e1_ref, e2_ref, o_ref, acc_ref, w_ref):
    nk = PEER_N_KEYS
    eb = pl.program_id(1)

    @pl.when(eb == 0)
    def _():
        acc_ref[...] = jnp.zeros_like(acc_ref)

    tm = xnt_ref.shape[1]
    ht = jnp.dot(u_ref[0], xnt_ref[...], preferred_element_type=F32)
    for al in range(A_BLOCK):
        for c in range(tm // LANES):
            cs = slice(c * LANES, (c + 1) * LANES)
            hs = ht[al * nk:(al + 1) * nk, cs]
            gate = None
            for h in range(PEER_HEADS):
                e2 = e2_ref[h, c]
                sel = jnp.where(e2 >= thr_ref[h, c, al:al + 1, :], e2, 0.0)
                term = e1_ref[h, c, al:al + 1, :] * sel
                gate = term if gate is None else gate + term
            w_ref[al * nk:(al + 1) * nk, cs] = (gate * (hs * (1.0 + lax.erf(hs * math.sqrt(0.5))))).astype(BF16)
    acc_ref[...] += jnp.dot(vt_ref[0, 0], w_ref[...], preferred_element_type=F32)

    @pl.when(eb == pl.num_programs(1) - 1)
    def _():
        o_ref[...] = x_ref[...] + acc_ref[...].T


def _experts(x2d, xnt, u, vt, thr, e1, e2, layer):
    t = x2d.shape[0]
    tm = min(TM_EXP, t)
    nk = PEER_N_KEYS
    te = A_BLOCK * nk
    nc = tm // LANES
    tab_a = pl.BlockSpec((PEER_HEADS, nc, A_BLOCK, LANES), lambda i, e: (0, i, e, 0))
    tab_b = pl.BlockSpec((PEER_HEADS, nc, nk, LANES), lambda i, e: (0, i, 0, 0))
    return pl.pallas_call(
        _expert_kernel,
        grid=(t // tm, nk // A_BLOCK),
        in_specs=[pl.BlockSpec((tm, D_MODEL), lambda i, e: (i, 0)),
                  pl.BlockSpec((D_MODEL, tm), lambda i, e: (0, i)),
                  pl.BlockSpec((1, te, D_MODEL), lambda i, e: (layer, e, 0)),
                  pl.BlockSpec((1, 1, D_MODEL, te), lambda i, e: (layer, e, 0, 0)),
                  tab_a, tab_a, tab_b],
        out_specs=pl.BlockSpec((tm, D_MODEL), lambda i, e: (i, 0)),
        out_shape=jax.ShapeDtypeStruct((t, D_MODEL), F32),
        scratch_shapes=[pltpu.VMEM((D_MODEL, tm), F32), pltpu.VMEM((te, tm), BF16)],
        compiler_params=_params(("parallel", "arbitrary")),
        name="peer_experts",
    )(x2d, xnt, u, vt, thr, e1, e2)


def _segment_mean_matrix(width, seg):
    idx = jnp.arange(width) // seg
    return (idx[:, None] == idx[None, :]).astype(F32) / seg


def _block_diag(w):
    g, c, _ = w.shape
    eye = jnp.eye(g, dtype=w.dtype)
    return (eye[:, None, :, None] * w[:, :, None, :]).reshape(g * c, g * c)


def kernel(x, mem, norm_mix, w_in, pool_w, pool_scale, conv_dw, conv_ln_g, conv_ln_b, conv_pw, diff_qk_norm,
           diff_lambda, diff_subln, sconv_w, mem_norm, w_mem_kv, mem_qk_norm, group_norm, w_out, norm_ffn,
           peer_wq, peer_subkeys, peer_u, peer_v):
    b, s, d = x.shape
    depth = w_in.shape[0]
    gw = GROUP_WIDTH
    seg32 = _segment_mean_matrix(gw, DIFF_HEAD_DIM)
    seg64 = _segment_mean_matrix(gw, MEM_HEAD_DIM)
    row = lambda a: a.reshape(1, -1)
    tile = lambda a, n: jnp.tile(a, n).reshape(1, -1)
    x2d = x.reshape(b * s, d)
    u_all = peer_u.astype(BF16)
    vt_all = peer_v.astype(BF16).reshape(depth, -1, A_BLOCK * PEER_N_KEYS, d).transpose(0, 1, 3, 2)
    for l in range(depth):
        lam_init = jnp.full((1, 1), 0.8 - 0.6 * math.exp(-0.3 * l), F32)
        upool, uconv, usc, umem, q, kt, v = _inproj(
            x2d, row(norm_mix[l]), w_in[l].astype(BF16), tile(diff_qk_norm[l, 0], 2 * DIFF_HEADS),
            tile(diff_qk_norm[l, 1], 2 * DIFF_HEADS), seg32, b, s)
        kmt, vm = _memkv(mem, row(mem_norm[l]), w_mem_kv[l].astype(BF16), tile(mem_qk_norm[l, 1], MEM_HEADS), seg64)
        ydiff = _diffattn(q, kt, v, diff_lambda[l], lam_init, tile(diff_subln[l], DIFF_HEADS), seg64, b, s)
        x2d = _mixers(x2d, upool, uconv, usc, umem, ydiff, kmt, vm,
                      _block_diag(pool_w[l]).astype(BF16), row(pool_scale[l]), conv_dw[l], row(conv_ln_g[l]),
                      row(conv_ln_b[l]), conv_pw[l].astype(BF16), sconv_w[l], tile(mem_qk_norm[l, 0], MEM_HEADS),
                      seg64, group_norm[l].reshape(N_GROUPS, gw), w_out[l].astype(BF16), b, s)
        xnt, thr, e1, e2 = _retrieval(
            x2d, row(norm_ffn[l]), peer_wq[l].T.astype(BF16),
            peer_subkeys[l].reshape(2 * PEER_HEADS, PEER_N_KEYS, -1).astype(BF16))
        x2d = _experts(x2d, xnt, u_all, vt_all, thr, e1, e2, l)
    return x2d.reshape(b, s, d)
```

```python
import functools
import math

import jax
import jax.numpy as jnp
from jax import lax
from jax.experimental import pallas as pl
from jax.experimental.pallas import tpu as pltpu

F32 = jnp.float32
BF16 = jnp.bfloat16

D_MODEL = 1024
GROUP_WIDTH = 256
N_GROUPS = 5
POOL_HALF_WINDOWS = (1, 2, 4, 8)
POOL_CH = 64
CONV_WIDTH = 31
SHORT_CONV_WIDTH = 3
DIFF_HEADS = 4
DIFF_HEAD_DIM = 32
MEM_HEADS = 4
MEM_HEAD_DIM = 64
PEER_HEADS = 8
PEER_N_KEYS = 128
PEER_TOPK = 16
EPS = 1e-6

LANES = 128
SUBLANES = 8
HALO = 16
VMEM_LIMIT = 56 * 1024 * 1024

TM_IN = 512
TQ_ATT = 256
TS_MIX = 512
TM_RET = 256
TM_EXP = 1024
A_BLOCK = 8
NEG = -1e30
LOG2E = 1.4426950408889634


def _params(sem):
    return pltpu.CompilerParams(dimension_semantics=sem, vmem_limit_bytes=VMEM_LIMIT)


def _rms(x, g):
    return x * lax.rsqrt(jnp.mean(x * x, axis=-1, keepdims=True) + EPS) * g


def _seg_rms(x, g, seg_mean):
    ms = jnp.dot(x * x, seg_mean, precision=lax.Precision.HIGHEST, preferred_element_type=F32)
    return x * lax.rsqrt(ms + EPS) * g


def _lane_range_mask(shape, lo, hi):
    lane = lax.broadcasted_iota(jnp.int32, shape, len(shape) - 1)
    return (lane >= lo) & (lane < hi)


def _inproj_kernel(x_ref, g_ref, w_ref, gq_ref, gk_ref, seg_ref,
                   upool_ref, uconv_ref, usc_ref, umem_ref, q_ref, kt_ref, v_ref):
    xb = _rms(x_ref[...], g_ref[...]).astype(BF16)

    def proj(lo, hi):
        return jnp.dot(xb, w_ref[:, lo:hi], preferred_element_type=F32)

    gw = GROUP_WIDTH
    upool_ref[...] = proj(0, gw)
    uconv_ref[...] = proj(gw, 3 * gw)
    q = proj(3 * gw, 4 * gw)
    k = proj(4 * gw, 5 * gw)
    v_ref[...] = proj(5 * gw, 6 * gw).astype(BF16)
    usc_ref[...] = proj(6 * gw, 9 * gw)
    umem_ref[...] = proj(9 * gw, 10 * gw)
    seg = seg_ref[...]
    q_ref[...] = (_seg_rms(q, gq_ref[...], seg) * (DIFF_HEAD_DIM ** -0.5 * LOG2E)).astype(BF16)
    kt_ref[0] = _seg_rms(k, gk_ref[...], seg).T.astype(BF16)


def _inproj(x2d, g, w, gq, gk, seg32, batch, seq):
    t = x2d.shape[0]
    tm = min(TM_IN, seq)
    nst = seq // tm
    gw = GROUP_WIDTH
    row = lambda c: pl.BlockSpec((tm, c), lambda i: (i, 0))
    full = lambda a: pl.BlockSpec(a.shape, lambda i: (0,) * a.ndim)
    return pl.pallas_call(
        _inproj_kernel,
        grid=(t // tm,),
        in_specs=[row(D_MODEL), full(g), full(w), full(gq), full(gk), full(seg32)],
        out_specs=[row(gw), row(2 * gw), row(3 * gw), row(gw), row(gw),
                   pl.BlockSpec((1, gw, tm), lambda i: (i // nst, 0, i % nst)), row(gw)],
        out_shape=[jax.ShapeDtypeStruct((t, gw), F32), jax.ShapeDtypeStruct((t, 2 * gw), F32),
                   jax.ShapeDtypeStruct((t, 3 * gw), F32), jax.ShapeDtypeStruct((t, gw), F32),
                   jax.ShapeDtypeStruct((t, gw), BF16), jax.ShapeDtypeStruct((batch, gw, seq), BF16),
                   jax.ShapeDtypeStruct((t, gw), BF16)],
        compiler_params=_params(("parallel",)),
        name="in_projection",
    )(x2d, g, w, gq, gk, seg32)


def _memkv_kernel(mem_ref, g_ref, w_ref, gk_ref, seg_ref, kt_ref, v_ref):
    mn = _rms(mem_ref[0], g_ref[...]).astype(BF16)
    kv = jnp.dot(mn, w_ref[...], preferred_element_type=F32)
    k = _seg_rms(kv[:, :GROUP_WIDTH], gk_ref[...], seg_ref[...])
    kt_ref[0] = k.T.astype(BF16)
    v_ref[0] = kv[:, GROUP_WIDTH:].astype(BF16)


def _memkv(mem, g, w, gk, seg64):
    b, m, _ = mem.shape
    gw = GROUP_WIDTH
    full = lambda a: pl.BlockSpec(a.shape, lambda i: (0,) * a.ndim)
    return pl.pallas_call(
        _memkv_kernel,
        grid=(b,),
        in_specs=[pl.BlockSpec((1, m, D_MODEL), lambda i: (i, 0, 0)), full(g), full(w), full(gk), full(seg64)],
        out_specs=[pl.BlockSpec((1, gw, m), lambda i: (i, 0, 0)), pl.BlockSpec((1, m, gw), lambda i: (i, 0, 0))],
        out_shape=[jax.ShapeDtypeStruct((b, gw, m), BF16), jax.ShapeDtypeStruct((b, m, gw), BF16)],
        compiler_params=_params(("parallel",)),
        name="memory_kv",
    )(mem, g, w, gk, seg64)


def _diffattn_kernel(q_ref, kt_ref, v_ref, lam_ref, laminit_ref, gsub_ref, seg_ref, o_ref):
    tq = q_ref.shape[0]
    seq = kt_ref.shape[2]
    lp = lam_ref[...]
    lam_init = laminit_ref[...]
    lam = (jnp.exp(jnp.sum(lp[0:1] * lp[1:2], axis=-1, keepdims=True))
           - jnp.exp(jnp.sum(lp[2:3] * lp[3:4], axis=-1, keepdims=True)) + lam_init)
    qpos = pl.program_id(1) * tq + lax.broadcasted_iota(jnp.int32, (tq, seq), 0)
    kpos = lax.broadcasted_iota(jnp.int32, (tq, seq), 1)
    dist = jnp.abs(qpos - kpos).astype(F32)
    q = q_ref[...]
    kt = kt_ref[0]
    v = v_ref[...]
    o = jnp.zeros((tq, GROUP_WIDTH), F32)
    half = DIFF_HEADS * DIFF_HEAD_DIM
    for h in range(DIFF_HEADS):
        slope = 2.0 ** (-8.0 * (h + 1) / DIFF_HEADS)
        bias = (slope * LOG2E) * dist
        vlo = h * 2 * DIFF_HEAD_DIM
        zlane = (vlo + 2 * DIFF_HEAD_DIM) % GROUP_WIDTH
        v_ones = jnp.where(_lane_range_mask(v.shape, zlane, zlane + 1), jnp.ones_like(v), v)
        outs = []
        for j in range(2):
            lo = j * half + h * DIFF_HEAD_DIM
            qm = jnp.where(_lane_range_mask(q.shape, lo, lo + DIFF_HEAD_DIM), q, jnp.zeros_like(q))
            s = jnp.dot(qm, kt, preferred_element_type=F32) - bias
            e = jnp.exp2(s - jnp.max(s, axis=-1, keepdims=True))
            ev = jnp.dot(e.astype(BF16), v_ones, preferred_element_type=F32)
            outs.append(ev * (1.0 / ev[:, zlane:zlane + 1]))
        oh = outs[0] - lam * outs[1]
        o = jnp.where(_lane_range_mask(o.shape, vlo, vlo + 2 * DIFF_HEAD_DIM), oh, o)
    o_ref[...] = _seg_rms(o, gsub_ref[...], seg_ref[...]) * (1.0 - lam_init)


def _diffattn(q, kt, v, lam_p, lam_init, gsub, seg64, batch, seq):
    t = q.shape[0]
    gw = GROUP_WIDTH
    tq = min(TQ_ATT, seq)
    nq = seq // tq
    full = lambda a: pl.BlockSpec(a.shape, lambda b, i: (0,) * a.ndim)
    return pl.pallas_call(
        _diffattn_kernel,
        grid=(batch, nq),
        in_specs=[pl.BlockSpec((tq, gw), lambda b, i: (b * nq + i, 0)),
                  pl.BlockSpec((1, gw, seq), lambda b, i: (b, 0, 0)),
                  pl.BlockSpec((seq, gw), lambda b, i: (b, 0)),
                  full(lam_p), full(lam_init), full(gsub), full(seg64)],
        out_specs=pl.BlockSpec((tq, gw), lambda b, i: (b * nq + i, 0)),
        out_shape=jax.ShapeDtypeStruct((t, gw), F32),
        compiler_params=_params(("parallel", "parallel")),
        name="diff_attention",
    )(q, kt, v, lam_p, lam_init, gsub, seg64)


def _mixer_kernel(x_ref, up_ref, up_prev, up_next, uc_ref, uc_prev, uc_next, us_ref, us_prev, us_next,
                  um_ref, yd_ref, kmt_ref, vm_ref,
                  poolw_ref, pools_ref, dw_ref, lng_ref, lnb_ref, pw_ref, sw_ref, gqm_ref, seg_ref,
                  gn_ref, wout_ref, o_ref, ext_pool, ext_conv, ext_sc, shift_conv, *, seq):
    ts = x_ref.shape[0]
    gw = GROUP_WIDTH
    nst = seq // ts
    it = pl.program_id(0) % nst
    keep_prev = jnp.where(it > 0, 1.0, 0.0)
    keep_next = jnp.where(it < nst - 1, 1.0, 0.0)

    def fill(ext, prev, main, nxt):
        ext[0:HALO, :] = prev * keep_prev
        ext[HALO:HALO + ts, :] = main
        ext[HALO + ts:HALO + ts + HALO, :] = nxt * keep_next

    def shifted(ext, k):
        return ext[HALO + k:HALO + k + ts, :]

    fill(ext_pool, up_prev[...], up_ref[...], up_next[...])
    u = up_ref[...]
    lane = lax.broadcasted_iota(jnp.int32, (ts, gw), 1)
    pos = it * ts + lax.broadcasted_iota(jnp.int32, (ts, gw), 0)
    wsum = jnp.zeros((ts, gw), F32)
    halfw = jnp.zeros((ts, gw), jnp.int32)
    run = None
    prev_half = 0
    for gi, hw in enumerate(POOL_HALF_WINDOWS):
        for k in list(range(-hw, -prev_half)) + list(range(prev_half, hw)):
            term = shifted(ext_pool, k)
            run = term if run is None else run + term
        prev_half = hw
        in_group = (lane >= gi * POOL_CH) & (lane < (gi + 1) * POOL_CH)
        wsum = jnp.where(in_group, run, wsum)
        halfw = jnp.where(in_group, hw, halfw)
    cnt = (jnp.minimum(pos + halfw, seq) - jnp.maximum(pos - halfw, 0)).astype(F32)
    pooled = wsum / cnt - u
    y_pool = jnp.dot(pooled.astype(BF16), poolw_ref[...], preferred_element_type=F32) * pools_ref[...]

    def glu(ucv):
        return ucv[:, :gw] * jax.nn.sigmoid(ucv[:, gw:])

    fill(ext_conv, glu(uc_prev[...]), glu(uc_ref[...]), glu(uc_next[...]))
    pad = CONV_WIDTH // 2
    for r in range(SUBLANES):
        shift_conv[r] = ext_conv[r:r + shift_conv.shape[1], :]
    conv = jnp.zeros((ts, gw), F32)
    for k in range(CONV_WIDTH):
        m, r = divmod(HALO - pad + k, SUBLANES)
        conv = conv + shift_conv[r, SUBLANES * m:SUBLANES * m + ts, :] * dw_ref[k:k + 1, :]
    mu = jnp.mean(conv, axis=-1, keepdims=True)
    cen = conv - mu
    var = jnp.mean(cen * cen, axis=-1, keepdims=True)
    hln = cen * lax.rsqrt(var + EPS) * lng_ref[...] + lnb_ref[...]
    hact = hln * jax.nn.sigmoid(hln)
    y_conv = jnp.dot(hact.astype(BF16), pw_ref[...], preferred_element_type=F32)

    def ch(usv):
        return usv[:, 2 * gw:] * usv[:, :gw]

    fill(ext_sc, ch(us_prev[...]), ch(us_ref[...]), ch(us_next[...]))
    spad = SHORT_CONV_WIDTH // 2
    sconv = jnp.zeros((ts, gw), F32)
    for k in range(SHORT_CONV_WIDTH):
        sconv = sconv + shifted(ext_sc, k - spad) * sw_ref[k:k + 1, :]
    y_sc = us_ref[:, gw:2 * gw] * sconv

    qn = (_seg_rms(um_ref[...], gqm_ref[...], seg_ref[...]) * (MEM_HEAD_DIM ** -0.5)).astype(BF16)
    kmt = kmt_ref[0]
    vm = vm_ref[0]
    y_mem = jnp.zeros((ts, gw), F32)
    for h in range(MEM_HEADS):
        hmask = _lane_range_mask((ts, gw), h * MEM_HEAD_DIM, (h + 1) * MEM_HEAD_DIM)
        qm = jnp.where(hmask, qn, jnp.zeros_like(qn))
        s = jnp.dot(qm, kmt, preferred_element_type=F32)
        e = jnp.exp(s - jnp.max(s, axis=-1, keepdims=True))
        p = (e * (1.0 / jnp.sum(e, axis=-1, keepdims=True))).astype(BF16)
        y_mem = jnp.where(hmask, jnp.dot(p, vm, preferred_element_type=F32), y_mem)

    acc = x_ref[...]
    for gi, y in enumerate((y_pool, y_conv, yd_ref[...], y_sc, y_mem)):
        yn = _rms(y, gn_ref[gi:gi + 1, :]).astype(BF16)
        acc = acc + jnp.dot(yn, wout_ref[gi * gw:(gi + 1) * gw, :], preferred_element_type=F32)
    o_ref[...] = acc


def _mixers(x2d, upool, uconv, usc, umem, ydiff, kmt, vm, poolw, pools, dw, lng, lnb, pw, sw, gqm, seg64,
            gn, wout, batch, seq):
    t = x2d.shape[0]
    gw = GROUP_WIDTH
    ts = min(TS_MIX, seq)
    nst = seq // ts
    r = ts // HALO
    last = t // HALO - 1
    row = lambda c: pl.BlockSpec((ts, c), lambda i: (i, 0))
    prev = lambda c: pl.BlockSpec((HALO, c), lambda i: (jnp.maximum(i * r - 1, 0), 0))
    nxt = lambda c: pl.BlockSpec((HALO, c), lambda i: (jnp.minimum((i + 1) * r, last), 0))
    full = lambda a: pl.BlockSpec(a.shape, lambda i: (0,) * a.ndim)
    perb = lambda a: pl.BlockSpec((1,) + a.shape[1:], lambda i: (i // nst, 0, 0))
    consts = (poolw, pools, dw, lng, lnb, pw, sw, gqm, seg64, gn, wout)
    return pl.pallas_call(
        functools.partial(_mixer_kernel, seq=seq),
        grid=(t // ts,),
        in_specs=[row(D_MODEL),
                  row(gw), prev(gw), nxt(gw),
                  row(2 * gw), prev(2 * gw), nxt(2 * gw),
                  row(3 * gw), prev(3 * gw), nxt(3 * gw),
                  row(gw), row(gw), perb(kmt), perb(vm)] + [full(a) for a in consts],
        out_specs=row(D_MODEL),
        out_shape=jax.ShapeDtypeStruct((t, D_MODEL), F32),
        scratch_shapes=[pltpu.VMEM((ts + 2 * HALO, gw), F32)] * 3
        + [pltpu.VMEM((SUBLANES, ts + 2 * HALO - SUBLANES, gw), F32)],
        compiler_params=_params(("parallel",)),
        name="mixers_out_projection",
    )(x2d, upool, upool, upool, uconv, uconv, uconv, usc, usc, usc, umem, ydiff, kmt, vm, *consts)


def _sorting_network(n):
    pairs = []
    p = 1
    while p < n:
        k = p
        while k >= 1:
            for j in range(k % p, n - k, 2 * k):
                for i in range(min(k, n - j - k)):
                    if (i + j) // (2 * p) == (i + j + k) // (2 * p):
                        pairs.append((i + j, i + j + k))
            k //= 2
        p *= 2
    return pairs


def _sort_blocks_descending(blocks):
    blocks = list(blocks)
    for i, j in _sorting_network(len(blocks)):
        hi, lo = jnp.maximum(blocks[i], blocks[j]), jnp.minimum(blocks[i], blocks[j])
        blocks[i], blocks[j] = hi, lo
    return blocks


def _pop_top_values(stack, singles, n):
    stack, singles = list(stack), list(singles)
    vals = []
    for k in range(n):
        top = stack[0]
        for blk in singles:
            top = jnp.maximum(top, blk)
        m = jnp.max(top, axis=0, keepdims=True)
        vals.append(m)
        keep = min(len(stack), n - k - 1)
        if keep == 0:
            break
        hit = stack[0] == m
        stack = [jnp.where(hit, stack[r + 1] if r + 1 < len(stack) else NEG, stack[r]) for r in range(keep)]
        singles = [jnp.where(blk == m, NEG, blk) for blk in singles]
    return vals


def _retrieval_kernel(x_ref, g_ref, wqt_ref, sk_ref, xnt_ref, thr_ref, e1_ref, e2_ref):
    tm = x_ref.shape[0]
    nk = PEER_N_KEYS
    n = PEER_TOPK + 1
    xb = _rms(x_ref[...], g_ref[...]).T.astype(BF16)
    xnt_ref[...] = xb
    qt = jnp.dot(wqt_ref[...], xb, preferred_element_type=F32)
    row_id = lax.broadcasted_iota(jnp.int32, (8, tm), 0)
    for h in range(PEER_HEADS):
        sc = []
        for j in range(2):
            r0 = (2 * h + j) * nk
            sc.append(jnp.dot(sk_ref[2 * h + j], qt[r0:r0 + nk].astype(BF16), preferred_element_type=F32))
        v1, v2 = (_pop_top_values(_sort_blocks_descending([s[r:r + 8] for r in range(0, nk, 8)]), [], n)
                  for s in sc)
        v2_blocks = []
        for r0 in range(0, n, 8):
            blk = jnp.full((8, tm), NEG, F32)
            for i in range(r0, min(r0 + 8, n)):
                blk = jnp.where(row_id == i - r0, v2[i], blk)
            v2_blocks.append(blk)
        stack = [v1[0] + v2_blocks[0]]
        for i in range(1, n):
            stack.append(v1[i] + jnp.where(row_id < n // (i + 1), v2_blocks[0], NEG))
        best = _pop_top_values(stack, [v1[0] + blk for blk in v2_blocks[1:]], n)
        z = jnp.zeros((1, tm), F32)
        for i in range(PEER_TOPK):
            z = z + jnp.exp(best[i] - best[0])
        half_inv_z = math.sqrt(0.5) / z
        tau = 0.5 * (best[PEER_TOPK - 1] + best[PEER_TOPK])
        thr = jnp.exp(tau - sc[0] - v2[0]) * half_inv_z
        e1 = jnp.exp(sc[0] - v1[0])
        e2 = jnp.exp(sc[1] - v2[0]) * half_inv_z
        for c in range(tm // LANES):
            cs = slice(c * LANES, (c + 1) * LANES)
            thr_ref[h, c] = thr[:, cs]
            e1_ref[h, c] = e1[:, cs]
            e2_ref[h, c] = e2[:, cs]


def _retrieval(x2d, g, wqt, sk):
    t = x2d.shape[0]
    tm = min(TM_RET, t)
    nk = PEER_N_KEYS
    full = lambda a: pl.BlockSpec(a.shape, lambda i: (0,) * a.ndim)
    tab = pl.BlockSpec((PEER_HEADS, tm // LANES, nk, LANES), lambda i: (0, i, 0, 0))
    tab_shape = jax.ShapeDtypeStruct((PEER_HEADS, t // LANES, nk, LANES), F32)
    return pl.pallas_call(
        _retrieval_kernel,
        grid=(t // tm,),
        in_specs=[pl.BlockSpec((tm, D_MODEL), lambda i: (i, 0)), full(g), full(wqt), full(sk)],
        out_specs=[pl.BlockSpec((D_MODEL, tm), lambda i: (0, i)), tab, tab, tab],
        out_shape=[jax.ShapeDtypeStruct((D_MODEL, t), BF16), tab_shape, tab_shape, tab_shape],
        compiler_params=_params(("parallel",)),
        name="peer_retrieval",
    )(x2d, g, wqt, sk)


def _expert_kernel(x_ref, xnt_ref, u_ref, vt_ref, thr_ref, e1_ref, e2_ref, o_ref, acc_ref, w_ref):
    nk = PEER_N_KEYS
    eb = pl.program_id(1)

    @pl.when(eb == 0)
    def _():
        acc_ref[...] = jnp.zeros_like(acc_ref)

    tm = xnt_ref.shape[1]
    ht = jnp.dot(u_ref[0], xnt_ref[...], preferred_element_type=F32)
    for al in range(A_BLOCK):
        for c in range(tm // LANES):
            cs = slice(c * LANES, (c + 1) * LANES)
            hs = ht[al * nk:(al + 1) * nk, cs]
            gate = None
            for h in range(PEER_HEADS):
                e2 = e2_ref[h, c]
                sel = jnp.where(e2 >= thr_ref[h, c, al:al + 1, :], e2, 0.0)
                term = e1_ref[h, c, al:al + 1, :] * sel
                gate = term if gate is None else gate + term
            w_ref[al * nk:(al + 1) * nk, cs] = (gate * (hs * (1.0 + lax.erf(hs)))).astype(BF16)
    acc_ref[...] += jnp.dot(vt_ref[0, 0], w_ref[...], preferred_element_type=F32)

    @pl.when(eb == pl.num_programs(1) - 1)
    def _():
        o_ref[...] = x_ref[...] + acc_ref[...].T


def _experts(x2d, xnt, u, vt, thr, e1, e2, layer):
    t = x2d.shape[0]
    tm = min(TM_EXP, t)
    nk = PEER_N_KEYS
    te = A_BLOCK * nk
    nc = tm // LANES
    tab_a = pl.BlockSpec((PEER_HEADS, nc, A_BLOCK, LANES), lambda i, e: (0, i, e, 0))
    tab_b = pl.BlockSpec((PEER_HEADS, nc, nk, LANES), lambda i, e: (0, i, 0, 0))
    return pl.pallas_call(
        _expert_kernel,
        grid=(t // tm, nk // A_BLOCK),
        in_specs=[pl.BlockSpec((tm, D_MODEL), lambda i, e: (i, 0)),
                  pl.BlockSpec((D_MODEL, tm), lambda i, e: (0, i)),
                  pl.BlockSpec((1, te, D_MODEL), lambda i, e: (layer, e, 0)),
                  pl.BlockSpec((1, 1, D_MODEL, te), lambda i, e: (layer, e, 0, 0)),
                  tab_a, tab_a, tab_b],
        out_specs=pl.BlockSpec((tm, D_MODEL), lambda i, e: (i, 0)),
        out_shape=jax.ShapeDtypeStruct((t, D_MODEL), F32),
        scratch_shapes=[pltpu.VMEM((D_MODEL, tm), F32), pltpu.VMEM((te, tm), BF16)],
        compiler_params=_params(("parallel", "arbitrary")),
        name="peer_experts",
    )(x2d, xnt, u, vt, thr, e1, e2)


def _segment_mean_matrix(width, seg):
    idx = jnp.arange(width) // seg
    return (idx[:, None] == idx[None, :]).astype(F32) / seg


def _block_diag(w):
    g, c, _ = w.shape
    eye = jnp.eye(g, dtype=w.dtype)
    return (eye[:, None, :, None] * w[:, :, None, :]).reshape(g * c, g * c)


def kernel(x, mem, norm_mix, w_in, pool_w, pool_scale, conv_dw, conv_ln_g, conv_ln_b, conv_pw, diff_qk_norm,
           diff_lambda, diff_subln, sconv_w, mem_norm, w_mem_kv, mem_qk_norm, group_norm, w_out, norm_ffn,
           peer_wq, peer_subkeys, peer_u, peer_v):
    b, s, d = x.shape
    depth = w_in.shape[0]
    gw = GROUP_WIDTH
    seg32 = _segment_mean_matrix(gw, DIFF_HEAD_DIM)
    seg64 = _segment_mean_matrix(gw, MEM_HEAD_DIM)
    row = lambda a: a.reshape(1, -1)
    tile = lambda a, n: jnp.tile(a, n).reshape(1, -1)
    x2d = x.reshape(b * s, d)
    u_all = (peer_u * math.sqrt(0.5)).astype(BF16)
    vt_all = peer_v.astype(BF16).reshape(depth, -1, A_BLOCK * PEER_N_KEYS, d).transpose(0, 1, 3, 2)
    for l in range(depth):
        lam_init = jnp.full((1, 1), 0.8 - 0.6 * math.exp(-0.3 * l), F32)
        upool, uconv, usc, umem, q, kt, v = _inproj(
            x2d, row(norm_mix[l]), w_in[l].astype(BF16), tile(diff_qk_norm[l, 0], 2 * DIFF_HEADS),
            tile(diff_qk_norm[l, 1], 2 * DIFF_HEADS), seg32, b, s)
        kmt, vm = _memkv(mem, row(mem_norm[l]), w_mem_kv[l].astype(BF16), tile(mem_qk_norm[l, 1], MEM_HEADS), seg64)
        ydiff = _diffattn(q, kt, v, diff_lambda[l], lam_init, tile(diff_subln[l], DIFF_HEADS), seg64, b, s)
        x2d = _mixers(x2d, upool, uconv, usc, umem, ydiff, kmt, vm,
                      _block_diag(pool_w[l]).astype(BF16), row(pool_scale[l]), conv_dw[l], row(conv_ln_g[l]),
                      row(conv_ln_b[l]), conv_pw[l].astype(BF16), sconv_w[l], tile(mem_qk_norm[l, 0], MEM_HEADS),
                      seg64, group_norm[l].reshape(N_GROUPS, gw), w_out[l].astype(BF16), b, s)
        xnt, thr, e1, e2 = _retrieval(
            x2d, row(norm_ffn[l]), peer_wq[l].T.astype(BF16),
            peer_subkeys[l].reshape(2 * PEER_HEADS, PEER_N_KEYS, -1).astype(BF16))
        x2d = _experts(x2d, xnt, u_all, vt_all, thr, e1, e2, l)
    return x2d.reshape(b, s, d)
```

```python
import functools
import math

import jax
import jax.numpy as jnp
import numpy as np
from jax import lax
from jax.experimental import pallas as pl
from jax.experimental.pallas import tpu as pltpu

F32 = jnp.float32
BF16 = jnp.bfloat16

D_MODEL = 1024
GROUP_WIDTH = 256
N_GROUPS = 5
POOL_HALF_WINDOWS = (1, 2, 4, 8)
POOL_CH = 64
CONV_WIDTH = 31
SHORT_CONV_WIDTH = 3
DIFF_HEADS = 4
DIFF_HEAD_DIM = 32
MEM_HEADS = 4
MEM_HEAD_DIM = 64
PEER_HEADS = 8
PEER_N_KEYS = 128
PEER_TOPK = 16
EPS = 1e-6

LANES = 128
SUBLANES = 8
HALO = 16
VMEM_LIMIT = 56 * 1024 * 1024

TM_IN = 512
TQ_ATT = 256
TS_MIX = 512
TM_RET = 256
TM_EXP = 1024
A_BLOCK = 8
NEG = -1e30
LOG2E = 1.4426950408889634


def _params(sem):
    return pltpu.CompilerParams(dimension_semantics=sem, vmem_limit_bytes=VMEM_LIMIT)


def _rms(x, g):
    return x * lax.rsqrt(jnp.mean(x * x, axis=-1, keepdims=True) + EPS) * g


def _seg_rms(x, g, seg_mean):
    ms = jnp.dot(x * x, seg_mean, precision=lax.Precision.HIGHEST, preferred_element_type=F32)
    return x * lax.rsqrt(ms + EPS) * g


def _lane_range_mask(shape, lo, hi):
    lane = lax.broadcasted_iota(jnp.int32, shape, len(shape) - 1)
    return (lane >= lo) & (lane < hi)


def _inproj_kernel(x_ref, g_ref, w_ref, gq_ref, gk_ref, seg_ref,
                   upool_ref, uconv_ref, usc_ref, umem_ref, q_ref, kt_ref, v_ref):
    xb = _rms(x_ref[...], g_ref[...]).astype(BF16)

    def proj(lo, hi):
        return jnp.dot(xb, w_ref[:, lo:hi], preferred_element_type=F32)

    gw = GROUP_WIDTH
    upool_ref[...] = proj(0, gw)
    uconv_ref[...] = proj(gw, 3 * gw)
    q = proj(3 * gw, 4 * gw)
    k = proj(4 * gw, 5 * gw)
    v_ref[...] = proj(5 * gw, 6 * gw).astype(BF16)
    usc_ref[...] = proj(6 * gw, 9 * gw)
    umem_ref[...] = proj(9 * gw, 10 * gw)
    seg = seg_ref[...]
    q_ref[...] = (_seg_rms(q, gq_ref[...], seg) * (DIFF_HEAD_DIM ** -0.5 * LOG2E)).astype(BF16)
    kt_ref[0] = _seg_rms(k, gk_ref[...], seg).T.astype(BF16)


def _inproj(x2d, g, w, gq, gk, seg32, batch, seq):
    t = x2d.shape[0]
    tm = min(TM_IN, seq)
    nst = seq // tm
    gw = GROUP_WIDTH
    row = lambda c: pl.BlockSpec((tm, c), lambda i: (i, 0))
    full = lambda a: pl.BlockSpec(a.shape, lambda i: (0,) * a.ndim)
    return pl.pallas_call(
        _inproj_kernel,
        grid=(t // tm,),
        in_specs=[row(D_MODEL), full(g), full(w), full(gq), full(gk), full(seg32)],
        out_specs=[row(gw), row(2 * gw), row(3 * gw), row(gw), row(gw),
                   pl.BlockSpec((1, gw, tm), lambda i: (i // nst, 0, i % nst)), row(gw)],
        out_shape=[jax.ShapeDtypeStruct((t, gw), F32), jax.ShapeDtypeStruct((t, 2 * gw), F32),
                   jax.ShapeDtypeStruct((t, 3 * gw), F32), jax.ShapeDtypeStruct((t, gw), F32),
                   jax.ShapeDtypeStruct((t, gw), BF16), jax.ShapeDtypeStruct((batch, gw, seq), BF16),
                   jax.ShapeDtypeStruct((t, gw), BF16)],
        compiler_params=_params(("parallel",)),
        name="in_projection",
    )(x2d, g, w, gq, gk, seg32)


def _memkv_kernel(mem_ref, g_ref, w_ref, gk_ref, seg_ref, kt_ref, v_ref):
    mn = _rms(mem_ref[0], g_ref[...]).astype(BF16)
    kv = jnp.dot(mn, w_ref[...], preferred_element_type=F32)
    k = _seg_rms(kv[:, :GROUP_WIDTH], gk_ref[...], seg_ref[...])
    kt_ref[0] = k.T.astype(BF16)
    v_ref[0] = kv[:, GROUP_WIDTH:].astype(BF16)


def _memkv(mem, g, w, gk, seg64):
    b, m, _ = mem.shape
    gw = GROUP_WIDTH
    full = lambda a: pl.BlockSpec(a.shape, lambda i: (0,) * a.ndim)
    return pl.pallas_call(
        _memkv_kernel,
        grid=(b,),
        in_specs=[pl.BlockSpec((1, m, D_MODEL), lambda i: (i, 0, 0)), full(g), full(w), full(gk), full(seg64)],
        out_specs=[pl.BlockSpec((1, gw, m), lambda i: (i, 0, 0)), pl.BlockSpec((1, m, gw), lambda i: (i, 0, 0))],
        out_shape=[jax.ShapeDtypeStruct((b, gw, m), BF16), jax.ShapeDtypeStruct((b, m, gw), BF16)],
        compiler_params=_params(("parallel",)),
        name="memory_kv",
    )(mem, g, w, gk, seg64)


def _bf16_parts(x, n):
    parts = []
    for _ in range(n):
        bits = np.array([x], np.float32).view(np.uint32)[0]
        bits = (bits + (((bits >> 16) & 1) + 0x7FFF)) & 0xFFFF0000
        p = float(np.array([bits], np.uint32).view(np.float32)[0])
        parts.append(p)
        x -= p
    return parts


def _by_index(idx, values):
    out = jnp.full(idx.shape, values[-1], F32)
    for i in range(len(values) - 2, -1, -1):
        out = jnp.where(idx == i, values[i], out)
    return out


POS_SPLIT = 64
N_LOG2E_PARTS = 4


def _diffattn_kernel(q_ref, kt_ref, v_ref, lam_ref, laminit_ref, gsub_ref, seg_ref, o_ref, s_ref):
    tq = q_ref.shape[0]
    seq = kt_ref.shape[2]
    lp = lam_ref[...]
    lam_init = laminit_ref[...]
    lam = (jnp.exp(jnp.sum(lp[0:1] * lp[1:2], axis=-1, keepdims=True))
           - jnp.exp(jnp.sum(lp[2:3] * lp[3:4], axis=-1, keepdims=True)) + lam_init)
    q0 = pl.multiple_of(pl.program_id(1) * tq, tq)
    q = q_ref[...]
    kt = kt_ref[0]
    v = v_ref[...]
    half = DIFF_HEADS * DIFF_HEAD_DIM
    npart = N_LOG2E_PARTS
    log2e = _bf16_parts(LOG2E, npart)
    arow = lax.broadcasted_iota(jnp.int32, (4 * npart, seq), 0)
    kpos = lax.broadcasted_iota(jnp.int32, (4 * npart, seq), 1)
    sign = jnp.where(kpos < q0, 1.0, jnp.where(kpos >= q0 + tq, -1.0, 0.0))
    k_hi = (kpos & -POS_SPLIT).astype(F32)
    k_lo = (kpos & (POS_SPLIT - 1)).astype(F32)
    k_aug = sign * jnp.where(arow < npart, k_hi, jnp.where(arow < 2 * npart, k_lo, _by_index(arow % npart, log2e)))
    k_aug = jnp.concatenate([k_aug.astype(BF16), jnp.zeros((half - 4 * npart, seq), BF16)], axis=0)
    alane = lax.broadcasted_iota(jnp.int32, (tq, half), 1)
    qpos = q0 + lax.broadcasted_iota(jnp.int32, (tq, half), 0)
    q_hi = (qpos & -POS_SPLIT).astype(F32)
    q_lo = (qpos & (POS_SPLIT - 1)).astype(F32)
    q_aug = jnp.where(alane < 2 * npart, _by_index(alane % npart, log2e),
                      jnp.where(alane < 3 * npart, -q_hi, jnp.where(alane < 4 * npart, -q_lo, 0.0)))
    local = jnp.abs(lax.broadcasted_iota(jnp.int32, (tq, tq), 0)
                    - lax.broadcasted_iota(jnp.int32, (tq, tq), 1)).astype(F32)
    o = jnp.zeros((tq, GROUP_WIDTH), F32)
    for h in range(DIFF_HEADS):
        slope = 2.0 ** (-8.0 * (h + 1) / DIFF_HEADS)
        q_aug_h = (slope * q_aug).astype(BF16)
        own_bias = (slope * LOG2E) * local
        vlo = h * 2 * DIFF_HEAD_DIM
        zlane = (vlo + 2 * DIFF_HEAD_DIM) % GROUP_WIDTH
        v_ones = jnp.where(_lane_range_mask(v.shape, zlane, zlane + 1), jnp.ones_like(v), v)
        outs = []
        for j in range(2):
            qj = q[:, j * half:(j + 1) * half]
            lo = h * DIFF_HEAD_DIM
            qm = jnp.where(_lane_range_mask(qj.shape, lo, lo + DIFF_HEAD_DIM), qj, jnp.zeros_like(qj))
            s_ref[j] = jnp.dot(jnp.concatenate([qm, q_aug_h], axis=1),
                               jnp.concatenate([kt[j * half:(j + 1) * half], k_aug], axis=0),
                               preferred_element_type=F32)
            s_ref[j, :, pl.ds(q0, tq)] = s_ref[j, :, pl.ds(q0, tq)] - own_bias
            s = s_ref[j]
            e = jnp.exp2(s - jnp.max(s, axis=-1, keepdims=True))
            ev = jnp.dot(e.astype(BF16), v_ones, preferred_element_type=F32)
            outs.append(ev * (1.0 / ev[:, zlane:zlane + 1]))
        oh = outs[0] - lam * outs[1]
        o = jnp.where(_lane_range_mask(o.shape, vlo, vlo + 2 * DIFF_HEAD_DIM), oh, o)
    o_ref[...] = _seg_rms(o, gsub_ref[...], seg_ref[...]) * (1.0 - lam_init)


def _diffattn(q, kt, v, lam_p, lam_init, gsub, seg64, batch, seq):
    t = q.shape[0]
    gw = GROUP_WIDTH
    tq = min(TQ_ATT, seq)
    nq = seq // tq
    full = lambda a: pl.BlockSpec(a.shape, lambda b, i: (0,) * a.ndim)
    return pl.pallas_call(
        _diffattn_kernel,
        grid=(batch, nq),
        in_specs=[pl.BlockSpec((tq, gw), lambda b, i: (b * nq + i, 0)),
                  pl.BlockSpec((1, gw, seq), lambda b, i: (b, 0, 0)),
                  pl.BlockSpec((seq, gw), lambda b, i: (b, 0)),
                  full(lam_p), full(lam_init), full(gsub), full(seg64)],
        out_specs=pl.BlockSpec((tq, gw), lambda b, i: (b * nq + i, 0)),
        out_shape=jax.ShapeDtypeStruct((t, gw), F32),
        scratch_shapes=[pltpu.VMEM((2, tq, seq), F32)],
        compiler_params=_params(("parallel", "parallel")),
        name="diff_attention",
    )(q, kt, v, lam_p, lam_init, gsub, seg64)


def _mixer_kernel(x_ref, up_ref, up_prev, up_next, uc_ref, uc_prev, uc_next, us_ref, us_prev, us_next,
                  um_ref, yd_ref, kmt_ref, vm_ref,
                  poolw_ref, pools_ref, dw_ref, lng_ref, lnb_ref, pw_ref, sw_ref, gqm_ref, seg_ref,
                  gn_ref, wout_ref, o_ref, ext_pool, ext_conv, ext_sc, shift_conv, *, seq):
    ts = x_ref.shape[0]
    gw = GROUP_WIDTH
    nst = seq // ts
    it = pl.program_id(0) % nst
    keep_prev = jnp.where(it > 0, 1.0, 0.0)
    keep_next = jnp.where(it < nst - 1, 1.0, 0.0)

    def fill(ext, prev, main, nxt):
        ext[0:HALO, :] = prev * keep_prev
        ext[HALO:HALO + ts, :] = main
        ext[HALO + ts:HALO + ts + HALO, :] = nxt * keep_next

    def shifted(ext, k):
        return ext[HALO + k:HALO + k + ts, :]

    fill(ext_pool, up_prev[...], up_ref[...], up_next[...])
    u = up_ref[...]
    lane = lax.broadcasted_iota(jnp.int32, (ts, gw), 1)
    pos = it * ts + lax.broadcasted_iota(jnp.int32, (ts, gw), 0)
    wsum = jnp.zeros((ts, gw), F32)
    halfw = jnp.zeros((ts, gw), jnp.int32)
    run = None
    prev_half = 0
    for gi, hw in enumerate(POOL_HALF_WINDOWS):
        for k in list(range(-hw, -prev_half)) + list(range(prev_half, hw)):
            term = shifted(ext_pool, k)
            run = term if run is None else run + term
        prev_half = hw
        in_group = (lane >= gi * POOL_CH) & (lane < (gi + 1) * POOL_CH)
        wsum = jnp.where(in_group, run, wsum)
        halfw = jnp.where(in_group, hw, halfw)
    cnt = (jnp.minimum(pos + halfw, seq) - jnp.maximum(pos - halfw, 0)).astype(F32)
    pooled = wsum / cnt - u
    y_pool = jnp.dot(pooled.astype(BF16), poolw_ref[...], preferred_element_type=F32) * pools_ref[...]

    def glu(ucv):
        return ucv[:, :gw] * jax.nn.sigmoid(ucv[:, gw:])

    fill(ext_conv, glu(uc_prev[...]), glu(uc_ref[...]), glu(uc_next[...]))
    pad = CONV_WIDTH // 2
    for r in range(SUBLANES):
        shift_conv[r] = ext_conv[r:r + shift_conv.shape[1], :]
    conv = jnp.zeros((ts, gw), F32)
    for k in range(CONV_WIDTH):
        m, r = divmod(HALO - pad + k, SUBLANES)
        conv = conv + shift_conv[r, SUBLANES * m:SUBLANES * m + ts, :] * dw_ref[k:k + 1, :]
    mu = jnp.mean(conv, axis=-1, keepdims=True)
    cen = conv - mu
    var = jnp.mean(cen * cen, axis=-1, keepdims=True)
    hln = cen * lax.rsqrt(var + EPS) * lng_ref[...] + lnb_ref[...]
    hact = hln * jax.nn.sigmoid(hln)
    y_conv = jnp.dot(hact.astype(BF16), pw_ref[...], preferred_element_type=F32)

    def ch(usv):
        return usv[:, 2 * gw:] * usv[:, :gw]

    fill(ext_sc, ch(us_prev[...]), ch(us_ref[...]), ch(us_next[...]))
    spad = SHORT_CONV_WIDTH // 2
    sconv = jnp.zeros((ts, gw), F32)
    for k in range(SHORT_CONV_WIDTH):
        sconv = sconv + shifted(ext_sc, k - spad) * sw_ref[k:k + 1, :]
    y_sc = us_ref[:, gw:2 * gw] * sconv

    qn = (_seg_rms(um_ref[...], gqm_ref[...], seg_ref[...]) * (MEM_HEAD_DIM ** -0.5)).astype(BF16)
    kmt = kmt_ref[0]
    vm = vm_ref[0]
    y_mem = jnp.zeros((ts, gw), F32)
    for h in range(MEM_HEADS):
        hmask = _lane_range_mask((ts, gw), h * MEM_HEAD_DIM, (h + 1) * MEM_HEAD_DIM)
        qm = jnp.where(hmask, qn, jnp.zeros_like(qn))
        s = jnp.dot(qm, kmt, preferred_element_type=F32)
        e = jnp.exp(s - jnp.max(s, axis=-1, keepdims=True))
        p = (e * (1.0 / jnp.sum(e, axis=-1, keepdims=True))).astype(BF16)
        y_mem = jnp.where(hmask, jnp.dot(p, vm, preferred_element_type=F32), y_mem)

    acc = x_ref[...]
    for gi, y in enumerate((y_pool, y_conv, yd_ref[...], y_sc, y_mem)):
        yn = _rms(y, gn_ref[gi:gi + 1, :]).astype(BF16)
        acc = acc + jnp.dot(yn, wout_ref[gi * gw:(gi + 1) * gw, :], preferred_element_type=F32)
    o_ref[...] = acc


def _mixers(x2d, upool, uconv, usc, umem, ydiff, kmt, vm, poolw, pools, dw, lng, lnb, pw, sw, gqm, seg64,
            gn, wout, batch, seq):
    t = x2d.shape[0]
    gw = GROUP_WIDTH
    ts = min(TS_MIX, seq)
    nst = seq // ts
    r = ts // HALO
    last = t // HALO - 1
    row = lambda c: pl.BlockSpec((ts, c), lambda i: (i, 0))
    prev = lambda c: pl.BlockSpec((HALO, c), lambda i: (jnp.maximum(i * r - 1, 0), 0))
    nxt = lambda c: pl.BlockSpec((HALO, c), lambda i: (jnp.minimum((i + 1) * r, last), 0))
    full = lambda a: pl.BlockSpec(a.shape, lambda i: (0,) * a.ndim)
    perb = lambda a: pl.BlockSpec((1,) + a.shape[1:], lambda i: (i // nst, 0, 0))
    consts = (poolw, pools, dw, lng, lnb, pw, sw, gqm, seg64, gn, wout)
    return pl.pallas_call(
        functools.partial(_mixer_kernel, seq=seq),
        grid=(t // ts,),
        in_specs=[row(D_MODEL),
                  row(gw), prev(gw), nxt(gw),
                  row(2 * gw), prev(2 * gw), nxt(2 * gw),
                  row(3 * gw), prev(3 * gw), nxt(3 * gw),
                  row(gw), row(gw), perb(kmt), perb(vm)] + [full(a) for a in consts],
        out_specs=row(D_MODEL),
        out_shape=jax.ShapeDtypeStruct((t, D_MODEL), F32),
        scratch_shapes=[pltpu.VMEM((ts + 2 * HALO, gw), F32)] * 3
        + [pltpu.VMEM((SUBLANES, ts + 2 * HALO - SUBLANES, gw), F32)],
        compiler_params=_params(("parallel",)),
        name="mixers_out_projection",
    )(x2d, upool, upool, upool, uconv, uconv, uconv, usc, usc, usc, umem, ydiff, kmt, vm, *consts)


def _sorting_network(n):
    pairs = []
    p = 1
    while p < n:
        k = p
        while k >= 1:
            for j in range(k % p, n - k, 2 * k):
                for i in range(min(k, n - j - k)):
                    if (i + j) // (2 * p) == (i + j + k) // (2 * p):
                        pairs.append((i + j, i + j + k))
            k //= 2
        p *= 2
    return pairs


def _sort_blocks_descending(blocks):
    blocks = list(blocks)
    for i, j in _sorting_network(len(blocks)):
        hi, lo = jnp.maximum(blocks[i], blocks[j]), jnp.minimum(blocks[i], blocks[j])
        blocks[i], blocks[j] = hi, lo
    return blocks


def _pop_top_values(stack, singles, n):
    stack, singles = list(stack), list(singles)
    vals = []
    for k in range(n):
        top = stack[0]
        for blk in singles:
            top = jnp.maximum(top, blk)
        m = jnp.max(top, axis=0, keepdims=True)
        vals.append(m)
        keep = min(len(stack), n - k - 1)
        if keep == 0:
            break
        hit = stack[0] == m
        stack = [jnp.where(hit, stack[r + 1] if r + 1 < len(stack) else NEG, stack[r]) for r in range(keep)]
        singles = [jnp.where(blk == m, NEG, blk) for blk in singles]
    return vals


def _retrieval_kernel(x_ref, g_ref, wqt_ref, sk_ref, xnt_ref, thr_ref, e1_ref, e2_ref):
    tm = x_ref.shape[0]
    nk = PEER_N_KEYS
    n = PEER_TOPK + 1
    xb = _rms(x_ref[...], g_ref[...]).T.astype(BF16)
    xnt_ref[...] = xb
    qt = jnp.dot(wqt_ref[...], xb, preferred_element_type=F32)
    row_id = lax.broadcasted_iota(jnp.int32, (8, tm), 0)
    for h in range(PEER_HEADS):
        sc = []
        for j in range(2):
            r0 = (2 * h + j) * nk
            sc.append(jnp.dot(sk_ref[2 * h + j], qt[r0:r0 + nk].astype(BF16), preferred_element_type=F32))
        v1, v2 = (_pop_top_values(_sort_blocks_descending([s[r:r + 8] for r in range(0, nk, 8)]), [], n)
                  for s in sc)
        v2_blocks = []
        for r0 in range(0, n, 8):
            blk = jnp.full((8, tm), NEG, F32)
            for i in range(r0, min(r0 + 8, n)):
                blk = jnp.where(row_id == i - r0, v2[i], blk)
            v2_blocks.append(blk)
        stack = [v1[0] + v2_blocks[0]]
        for i in range(1, n):
            stack.append(v1[i] + jnp.where(row_id < n // (i + 1), v2_blocks[0], NEG))
        best = _pop_top_values(stack, [v1[0] + blk for blk in v2_blocks[1:]], n)
        z = jnp.zeros((1, tm), F32)
        for i in range(PEER_TOPK):
            z = z + jnp.exp(best[i] - best[0])
        half_inv_z = math.sqrt(0.5) / z
        tau = 0.5 * (best[PEER_TOPK - 1] + best[PEER_TOPK])
        thr = jnp.exp(tau - sc[0] - v2[0]) * half_inv_z
        e1 = jnp.exp(sc[0] - v1[0])
        e2 = jnp.exp(sc[1] - v2[0]) * half_inv_z
        for c in range(tm // LANES):
            cs = slice(c * LANES, (c + 1) * LANES)
            thr_ref[h, c] = thr[:, cs]
            e1_ref[h, c] = e1[:, cs]
            e2_ref[h, c] = e2[:, cs]


def _retrieval(x2d, g, wqt, sk):
    t = x2d.shape[0]
    tm = min(TM_RET, t)
    nk = PEER_N_KEYS
    full = lambda a: pl.BlockSpec(a.shape, lambda i: (0,) * a.ndim)
    tab = pl.BlockSpec((PEER_HEADS, tm // LANES, nk, LANES), lambda i: (0, i, 0, 0))
    tab_shape = jax.ShapeDtypeStruct((PEER_HEADS, t // LANES, nk, LANES), F32)
    return pl.pallas_call(
        _retrieval_kernel,
        grid=(t // tm,),
        in_specs=[pl.BlockSpec((tm, D_MODEL), lambda i: (i, 0)), full(g), full(wqt), full(sk)],
        out_specs=[pl.BlockSpec((D_MODEL, tm), lambda i: (0, i)), tab, tab, tab],
        out_shape=[jax.ShapeDtypeStruct((D_MODEL, t), BF16), tab_shape, tab_shape, tab_shape],
        compiler_params=_params(("parallel",)),
        name="peer_retrieval",
    )(x2d, g, wqt, sk)


def _expert_kernel(x_ref, xnt_ref, u_ref, vt_ref, thr_ref, e1_ref, e2_ref, o_ref, acc_ref, w_ref):
    nk = PEER_N_KEYS
    eb = pl.program_id(1)

    @pl.when(eb == 0)
    def _():
        acc_ref[...] = jnp.zeros_like(acc_ref)

    tm = xnt_ref.shape[1]
    ht = jnp.dot(u_ref[0], xnt_ref[...], preferred_element_type=F32)
    for al in range(A_BLOCK):
        for c in range(tm // LANES):
            cs = slice(c * LANES, (c + 1) * LANES)
            hs = ht[al * nk:(al + 1) * nk, cs]
            gate = None
            for h in range(PEER_HEADS):
                e2 = e2_ref[h, c]
                sel = jnp.where(e2 >= thr_ref[h, c, al:al + 1, :], e2, 0.0)
                term = e1_ref[h, c, al:al + 1, :] * sel
                gate = term if gate is None else gate + term
            w_ref[al * nk:(al + 1) * nk, cs] = (gate * (hs * (1.0 + lax.erf(hs)))).astype(BF16)
    acc_ref[...] += jnp.dot(vt_ref[0, 0], w_ref[...], preferred_element_type=F32)

    @pl.when(eb == pl.num_programs(1) - 1)
    def _():
        o_ref[...] = x_ref[...] + acc_ref[...].T


def _experts(x2d, xnt, u, vt, thr, e1, e2, layer):
    t = x2d.shape[0]
    tm = min(TM_EXP, t)
    nk = PEER_N_KEYS
    te = A_BLOCK * nk
    nc = tm // LANES
    tab_a = pl.BlockSpec((PEER_HEADS, nc, A_BLOCK, LANES), lambda i, e: (0, i, e, 0))
    tab_b = pl.BlockSpec((PEER_HEADS, nc, nk, LANES), lambda i, e: (0, i, 0, 0))
    return pl.pallas_call(
        _expert_kernel,
        grid=(t // tm, nk // A_BLOCK),
        in_specs=[pl.BlockSpec((tm, D_MODEL), lambda i, e: (i, 0)),
                  pl.BlockSpec((D_MODEL, tm), lambda i, e: (0, i)),
                  pl.BlockSpec((1, te, D_MODEL), lambda i, e: (layer, e, 0)),
                  pl.BlockSpec((1, 1, D_MODEL, te), lambda i, e: (layer, e, 0, 0)),
                  tab_a, tab_a, tab_b],
        out_specs=pl.BlockSpec((tm, D_MODEL), lambda i, e: (i, 0)),
        out_shape=jax.ShapeDtypeStruct((t, D_MODEL), F32),
        scratch_shapes=[pltpu.VMEM((D_MODEL, tm), F32), pltpu.VMEM((te, tm), BF16)],
        compiler_params=_params(("parallel", "arbitrary")),
        name="peer_experts",
    )(x2d, xnt, u, vt, thr, e1, e2)


def _segment_mean_matrix(width, seg):
    idx = jnp.arange(width) // seg
    return (idx[:, None] == idx[None, :]).astype(F32) / seg


def _block_diag(w):
    g, c, _ = w.shape
    eye = jnp.eye(g, dtype=w.dtype)
    return (eye[:, None, :, None] * w[:, :, None, :]).reshape(g * c, g * c)


def kernel(x, mem, norm_mix, w_in, pool_w, pool_scale, conv_dw, conv_ln_g, conv_ln_b, conv_pw, diff_qk_norm,
           diff_lambda, diff_subln, sconv_w, mem_norm, w_mem_kv, mem_qk_norm, group_norm, w_out, norm_ffn,
           peer_wq, peer_subkeys, peer_u, peer_v):
    b, s, d = x.shape
    depth = w_in.shape[0]
    gw = GROUP_WIDTH
    seg32 = _segment_mean_matrix(gw, DIFF_HEAD_DIM)
    seg64 = _segment_mean_matrix(gw, MEM_HEAD_DIM)
    row = lambda a: a.reshape(1, -1)
    tile = lambda a, n: jnp.tile(a, n).reshape(1, -1)
    x2d = x.reshape(b * s, d)
    u_all = (peer_u * math.sqrt(0.5)).astype(BF16)
    vt_all = peer_v.astype(BF16).reshape(depth, -1, A_BLOCK * PEER_N_KEYS, d).transpose(0, 1, 3, 2)
    for l in range(depth):
        lam_init = jnp.full((1, 1), 0.8 - 0.6 * math.exp(-0.3 * l), F32)
        upool, uconv, usc, umem, q, kt, v = _inproj(
            x2d, row(norm_mix[l]), w_in[l].astype(BF16), tile(diff_qk_norm[l, 0], 2 * DIFF_HEADS),
            tile(diff_qk_norm[l, 1], 2 * DIFF_HEADS), seg32, b, s)
        kmt, vm = _memkv(mem, row(mem_norm[l]), w_mem_kv[l].astype(BF16), tile(mem_qk_norm[l, 1], MEM_HEADS), seg64)
        ydiff = _diffattn(q, kt, v, diff_lambda[l], lam_init, tile(diff_subln[l], DIFF_HEADS), seg64, b, s)
        x2d = _mixers(x2d, upool, uconv, usc, umem, ydiff, kmt, vm,
                      _block_diag(pool_w[l]).astype(BF16), row(pool_scale[l]), conv_dw[l], row(conv_ln_g[l]),
                      row(conv_ln_b[l]), conv_pw[l].astype(BF16), sconv_w[l], tile(mem_qk_norm[l, 0], MEM_HEADS),
                      seg64, group_norm[l].reshape(N_GROUPS, gw), w_out[l].astype(BF16), b, s)
        xnt, thr, e1, e2 = _retrieval(
            x2d, row(norm_ffn[l]), peer_wq[l].T.astype(BF16),
            peer_subkeys[l].reshape(2 * PEER_HEADS, PEER_N_KEYS, -1).astype(BF16))
        x2d = _experts(x2d, xnt, u_all, vt_all, thr, e1, e2, l)
    return x2d.reshape(b, s, d)
```

```python
import functools
import math

import jax
import jax.numpy as jnp
import numpy as np
from jax import lax
from jax.experimental import pallas as pl
from jax.experimental.pallas import tpu as pltpu

F32 = jnp.float32
BF16 = jnp.bfloat16

D_MODEL = 1024
GROUP_WIDTH = 256
N_GROUPS = 5
POOL_HALF_WINDOWS = (1, 2, 4, 8)
POOL_CH = 64
CONV_WIDTH = 31
SHORT_CONV_WIDTH = 3
DIFF_HEADS = 4
DIFF_HEAD_DIM = 32
MEM_HEADS = 4
MEM_HEAD_DIM = 64
PEER_HEADS = 8
PEER_N_KEYS = 128
PEER_TOPK = 16
EPS = 1e-6

LANES = 128
SUBLANES = 8
HALO = 16
VMEM_LIMIT = 56 * 1024 * 1024

TM_IN = 512
TQ_ATT = 256
TS_MIX = 512
TM_RET = 256
TM_EXP = 1024
A_BLOCK = 8
NEG = -1e30
LOG2E = 1.4426950408889634


def _params(sem):
    return pltpu.CompilerParams(dimension_semantics=sem, vmem_limit_bytes=VMEM_LIMIT)


def _rms(x, g):
    return x * lax.rsqrt(jnp.mean(x * x, axis=-1, keepdims=True) + EPS) * g


def _seg_rms(x, g, seg_mean):
    ms = jnp.dot(x * x, seg_mean, precision=lax.Precision.HIGHEST, preferred_element_type=F32)
    return x * lax.rsqrt(ms + EPS) * g


def _lane_range_mask(shape, lo, hi):
    lane = lax.broadcasted_iota(jnp.int32, shape, len(shape) - 1)
    return (lane >= lo) & (lane < hi)


def _inproj_kernel(x_ref, g_ref, w_ref, gq_ref, gk_ref, seg_ref,
                   upool_ref, uconv_ref, usc_ref, umem_ref, q_ref, kt_ref, v_ref):
    xb = _rms(x_ref[...], g_ref[...]).astype(BF16)

    def proj(lo, hi):
        return jnp.dot(xb, w_ref[:, lo:hi], preferred_element_type=F32)

    gw = GROUP_WIDTH
    upool_ref[...] = proj(0, gw)
    uconv_ref[...] = proj(gw, 3 * gw)
    q = proj(3 * gw, 4 * gw)
    k = proj(4 * gw, 5 * gw)
    v_ref[...] = proj(5 * gw, 6 * gw).astype(BF16)
    usc_ref[...] = proj(6 * gw, 9 * gw)
    umem_ref[...] = proj(9 * gw, 10 * gw)
    seg = seg_ref[...]
    q_ref[...] = (_seg_rms(q, gq_ref[...], seg) * (DIFF_HEAD_DIM ** -0.5 * LOG2E)).astype(BF16)
    kt_ref[0] = _seg_rms(k, gk_ref[...], seg).T.astype(BF16)


def _inproj(x2d, g, w, gq, gk, seg32, batch, seq):
    t = x2d.shape[0]
    tm = min(TM_IN, seq)
    nst = seq // tm
    gw = GROUP_WIDTH
    row = lambda c: pl.BlockSpec((tm, c), lambda i: (i, 0))
    full = lambda a: pl.BlockSpec(a.shape, lambda i: (0,) * a.ndim)
    return pl.pallas_call(
        _inproj_kernel,
        grid=(t // tm,),
        in_specs=[row(D_MODEL), full(g), full(w), full(gq), full(gk), full(seg32)],
        out_specs=[row(gw), row(2 * gw), row(3 * gw), row(gw), row(gw),
                   pl.BlockSpec((1, gw, tm), lambda i: (i // nst, 0, i % nst)), row(gw)],
        out_shape=[jax.ShapeDtypeStruct((t, gw), F32), jax.ShapeDtypeStruct((t, 2 * gw), F32),
                   jax.ShapeDtypeStruct((t, 3 * gw), F32), jax.ShapeDtypeStruct((t, gw), F32),
                   jax.ShapeDtypeStruct((t, gw), BF16), jax.ShapeDtypeStruct((batch, gw, seq), BF16),
                   jax.ShapeDtypeStruct((t, gw), BF16)],
        compiler_params=_params(("parallel",)),
        name="in_projection",
    )(x2d, g, w, gq, gk, seg32)


def _memkv_kernel(mem_ref, g_ref, w_ref, gk_ref, seg_ref, kt_ref, v_ref):
    mn = _rms(mem_ref[0], g_ref[...]).astype(BF16)
    kv = jnp.dot(mn, w_ref[...], preferred_element_type=F32)
    k = _seg_rms(kv[:, :GROUP_WIDTH], gk_ref[...], seg_ref[...])
    kt_ref[0] = k.T.astype(BF16)
    v_ref[0] = kv[:, GROUP_WIDTH:].astype(BF16)


def _memkv(mem, g, w, gk, seg64):
    b, m, _ = mem.shape
    gw = GROUP_WIDTH
    full = lambda a: pl.BlockSpec(a.shape, lambda i: (0,) * a.ndim)
    return pl.pallas_call(
        _memkv_kernel,
        grid=(b,),
        in_specs=[pl.BlockSpec((1, m, D_MODEL), lambda i: (i, 0, 0)), full(g), full(w), full(gk), full(seg64)],
        out_specs=[pl.BlockSpec((1, gw, m), lambda i: (i, 0, 0)), pl.BlockSpec((1, m, gw), lambda i: (i, 0, 0))],
        out_shape=[jax.ShapeDtypeStruct((b, gw, m), BF16), jax.ShapeDtypeStruct((b, m, gw), BF16)],
        compiler_params=_params(("parallel",)),
        name="memory_kv",
    )(mem, g, w, gk, seg64)


def _bf16_parts(x, n):
    parts = []
    for _ in range(n):
        bits = np.array([x], np.float32).view(np.uint32)[0]
        bits = (bits + (((bits >> 16) & 1) + 0x7FFF)) & 0xFFFF0000
        p = float(np.array([bits], np.uint32).view(np.float32)[0])
        parts.append(p)
        x -= p
    return parts


def _by_index(idx, values):
    out = jnp.full(idx.shape, values[-1], F32)
    for i in range(len(values) - 2, -1, -1):
        out = jnp.where(idx == i, values[i], out)
    return out


POS_SPLIT = 64
N_LOG2E_PARTS = 4


def _diffattn_kernel(q_ref, kt_ref, v_ref, lam_ref, laminit_ref, gsub_ref, seg_ref, o_ref, s_ref):
    tq = q_ref.shape[0]
    seq = kt_ref.shape[2]
    lp = lam_ref[...]
    lam_init = laminit_ref[...]
    lam = (jnp.exp(jnp.sum(lp[0:1] * lp[1:2], axis=-1, keepdims=True))
           - jnp.exp(jnp.sum(lp[2:3] * lp[3:4], axis=-1, keepdims=True)) + lam_init)
    q0 = pl.multiple_of(pl.program_id(1) * tq, tq)
    q = q_ref[...]
    kt = kt_ref[0]
    v = v_ref[...]
    half = DIFF_HEADS * DIFF_HEAD_DIM
    npart = N_LOG2E_PARTS
    log2e = _bf16_parts(LOG2E, npart)
    arow = lax.broadcasted_iota(jnp.int32, (4 * npart, seq), 0)
    kpos = lax.broadcasted_iota(jnp.int32, (4 * npart, seq), 1)
    sign = jnp.where(kpos < q0, 1.0, jnp.where(kpos >= q0 + tq, -1.0, 0.0))
    k_hi = (kpos & -POS_SPLIT).astype(F32)
    k_lo = (kpos & (POS_SPLIT - 1)).astype(F32)
    k_aug = sign * jnp.where(arow < npart, k_hi, jnp.where(arow < 2 * npart, k_lo, _by_index(arow % npart, log2e)))
    k_aug = jnp.concatenate([k_aug.astype(BF16), jnp.zeros((half - 4 * npart, seq), BF16)], axis=0)
    alane = lax.broadcasted_iota(jnp.int32, (tq, half), 1)
    qpos = q0 + lax.broadcasted_iota(jnp.int32, (tq, half), 0)
    q_hi = (qpos & -POS_SPLIT).astype(F32)
    q_lo = (qpos & (POS_SPLIT - 1)).astype(F32)
    q_aug = jnp.where(alane < 2 * npart, _by_index(alane % npart, log2e),
                      jnp.where(alane < 3 * npart, -q_hi, jnp.where(alane < 4 * npart, -q_lo, 0.0)))
    local = jnp.abs(lax.broadcasted_iota(jnp.int32, (tq, tq), 0)
                    - lax.broadcasted_iota(jnp.int32, (tq, tq), 1)).astype(F32)
    o = jnp.zeros((tq, GROUP_WIDTH), F32)
    for h in range(DIFF_HEADS):
        slope = 2.0 ** (-8.0 * (h + 1) / DIFF_HEADS)
        q_aug_h = (slope * q_aug).astype(BF16)
        own_bias = (slope * LOG2E) * local
        vlo = h * 2 * DIFF_HEAD_DIM
        zlane = (vlo + 2 * DIFF_HEAD_DIM) % GROUP_WIDTH
        v_ones = jnp.where(_lane_range_mask(v.shape, zlane, zlane + 1), jnp.ones_like(v), v)
        outs = []
        for j in range(2):
            qj = q[:, j * half:(j + 1) * half]
            lo = h * DIFF_HEAD_DIM
            qm = jnp.where(_lane_range_mask(qj.shape, lo, lo + DIFF_HEAD_DIM), qj, jnp.zeros_like(qj))
            s_ref[j] = jnp.dot(jnp.concatenate([qm, q_aug_h], axis=1),
                               jnp.concatenate([kt[j * half:(j + 1) * half], k_aug], axis=0),
                               preferred_element_type=F32)
            s_ref[j, :, pl.ds(q0, tq)] = s_ref[j, :, pl.ds(q0, tq)] - own_bias
            s = s_ref[j]
            e = jnp.exp2(s - jnp.max(s, axis=-1, keepdims=True))
            ev = jnp.dot(e.astype(BF16), v_ones, preferred_element_type=F32)
            outs.append(ev * (1.0 / ev[:, zlane:zlane + 1]))
        oh = outs[0] - lam * outs[1]
        o = jnp.where(_lane_range_mask(o.shape, vlo, vlo + 2 * DIFF_HEAD_DIM), oh, o)
    o_ref[...] = _seg_rms(o, gsub_ref[...], seg_ref[...]) * (1.0 - lam_init)


def _diffattn(q, kt, v, lam_p, lam_init, gsub, seg64, batch, seq):
    t = q.shape[0]
    gw = GROUP_WIDTH
    tq = min(TQ_ATT, seq)
    nq = seq // tq
    full = lambda a: pl.BlockSpec(a.shape, lambda b, i: (0,) * a.ndim)
    return pl.pallas_call(
        _diffattn_kernel,
        grid=(batch, nq),
        in_specs=[pl.BlockSpec((tq, gw), lambda b, i: (b * nq + i, 0)),
                  pl.BlockSpec((1, gw, seq), lambda b, i: (b, 0, 0)),
                  pl.BlockSpec((seq, gw), lambda b, i: (b, 0)),
                  full(lam_p), full(lam_init), full(gsub), full(seg64)],
        out_specs=pl.BlockSpec((tq, gw), lambda b, i: (b * nq + i, 0)),
        out_shape=jax.ShapeDtypeStruct((t, gw), F32),
        scratch_shapes=[pltpu.VMEM((2, tq, seq), F32)],
        compiler_params=_params(("parallel", "parallel")),
        name="diff_attention",
    )(q, kt, v, lam_p, lam_init, gsub, seg64)


def _mixer_kernel(x_ref, up_ref, up_prev, up_next, uc_ref, uc_prev, uc_next, us_ref, us_prev, us_next,
                  um_ref, yd_ref, kmt_ref, vm_ref,
                  poolw_ref, pools_ref, dw_ref, lng_ref, lnb_ref, pw_ref, sw_ref, gqm_ref, seg_ref,
                  gn_ref, wout_ref, o_ref, ext_pool, ext_conv, ext_sc, shift_conv, *, seq):
    ts = x_ref.shape[0]
    gw = GROUP_WIDTH
    nst = seq // ts
    it = pl.program_id(0) % nst
    keep_prev = jnp.where(it > 0, 1.0, 0.0)
    keep_next = jnp.where(it < nst - 1, 1.0, 0.0)

    def fill(ext, prev, main, nxt):
        ext[0:HALO, :] = prev * keep_prev
        ext[HALO:HALO + ts, :] = main
        ext[HALO + ts:HALO + ts + HALO, :] = nxt * keep_next

    def shifted(ext, k):
        return ext[HALO + k:HALO + k + ts, :]

    fill(ext_pool, up_prev[...], up_ref[...], up_next[...])
    u = up_ref[...]
    lane = lax.broadcasted_iota(jnp.int32, (ts, gw), 1)
    pos = it * ts + lax.broadcasted_iota(jnp.int32, (ts, gw), 0)
    wsum = jnp.zeros((ts, gw), F32)
    halfw = jnp.zeros((ts, gw), jnp.int32)
    run = None
    prev_half = 0
    for gi, hw in enumerate(POOL_HALF_WINDOWS):
        for k in list(range(-hw, -prev_half)) + list(range(prev_half, hw)):
            term = shifted(ext_pool, k)
            run = term if run is None else run + term
        prev_half = hw
        in_group = (lane >= gi * POOL_CH) & (lane < (gi + 1) * POOL_CH)
        wsum = jnp.where(in_group, run, wsum)
        halfw = jnp.where(in_group, hw, halfw)
    cnt = (jnp.minimum(pos + halfw, seq) - jnp.maximum(pos - halfw, 0)).astype(F32)
    pooled = wsum / cnt - u
    y_pool = jnp.dot(pooled.astype(BF16), poolw_ref[...], preferred_element_type=F32) * pools_ref[...]

    def glu(ucv):
        return ucv[:, :gw] * jax.nn.sigmoid(ucv[:, gw:])

    fill(ext_conv, glu(uc_prev[...]), glu(uc_ref[...]), glu(uc_next[...]))
    pad = CONV_WIDTH // 2
    for r in range(SUBLANES):
        shift_conv[r] = ext_conv[r:r + shift_conv.shape[1], :]
    conv = jnp.zeros((ts, gw), F32)
    for k in range(CONV_WIDTH):
        m, r = divmod(HALO - pad + k, SUBLANES)
        conv = conv + shift_conv[r, SUBLANES * m:SUBLANES * m + ts, :] * dw_ref[k:k + 1, :]
    mu = jnp.mean(conv, axis=-1, keepdims=True)
    cen = conv - mu
    var = jnp.mean(cen * cen, axis=-1, keepdims=True)
    hln = cen * lax.rsqrt(var + EPS) * lng_ref[...] + lnb_ref[...]
    hact = hln * jax.nn.sigmoid(hln)
    y_conv = jnp.dot(hact.astype(BF16), pw_ref[...], preferred_element_type=F32)

    def ch(usv):
        return usv[:, 2 * gw:] * usv[:, :gw]

    fill(ext_sc, ch(us_prev[...]), ch(us_ref[...]), ch(us_next[...]))
    spad = SHORT_CONV_WIDTH // 2
    sconv = jnp.zeros((ts, gw), F32)
    for k in range(SHORT_CONV_WIDTH):
        sconv = sconv + shifted(ext_sc, k - spad) * sw_ref[k:k + 1, :]
    y_sc = us_ref[:, gw:2 * gw] * sconv

    qn = (_seg_rms(um_ref[...], gqm_ref[...], seg_ref[...]) * (MEM_HEAD_DIM ** -0.5)).astype(BF16)
    kmt = kmt_ref[0]
    vm = vm_ref[0]
    y_mem = jnp.zeros((ts, gw), F32)
    for h in range(MEM_HEADS):
        hmask = _lane_range_mask((ts, gw), h * MEM_HEAD_DIM, (h + 1) * MEM_HEAD_DIM)
        qm = jnp.where(hmask, qn, jnp.zeros_like(qn))
        s = jnp.dot(qm, kmt, preferred_element_type=F32)
        e = jnp.exp(s - jnp.max(s, axis=-1, keepdims=True))
        p = (e * (1.0 / jnp.sum(e, axis=-1, keepdims=True))).astype(BF16)
        y_mem = jnp.where(hmask, jnp.dot(p, vm, preferred_element_type=F32), y_mem)

    acc = x_ref[...]
    for gi, y in enumerate((y_pool, y_conv, yd_ref[...], y_sc, y_mem)):
        yn = _rms(y, gn_ref[gi:gi + 1, :]).astype(BF16)
        acc = acc + jnp.dot(yn, wout_ref[gi * gw:(gi + 1) * gw, :], preferred_element_type=F32)
    o_ref[...] = acc


def _mixers(x2d, upool, uconv, usc, umem, ydiff, kmt, vm, poolw, pools, dw, lng, lnb, pw, sw, gqm, seg64,
            gn, wout, batch, seq):
    t = x2d.shape[0]
    gw = GROUP_WIDTH
    ts = min(TS_MIX, seq)
    nst = seq // ts
    r = ts // HALO
    last = t // HALO - 1
    row = lambda c: pl.BlockSpec((ts, c), lambda i: (i, 0))
    prev = lambda c: pl.BlockSpec((HALO, c), lambda i: (jnp.maximum(i * r - 1, 0), 0))
    nxt = lambda c: pl.BlockSpec((HALO, c), lambda i: (jnp.minimum((i + 1) * r, last), 0))
    full = lambda a: pl.BlockSpec(a.shape, lambda i: (0,) * a.ndim)
    perb = lambda a: pl.BlockSpec((1,) + a.shape[1:], lambda i: (i // nst, 0, 0))
    consts = (poolw, pools, dw, lng, lnb, pw, sw, gqm, seg64, gn, wout)
    return pl.pallas_call(
        functools.partial(_mixer_kernel, seq=seq),
        grid=(t // ts,),
        in_specs=[row(D_MODEL),
                  row(gw), prev(gw), nxt(gw),
                  row(2 * gw), prev(2 * gw), nxt(2 * gw),
                  row(3 * gw), prev(3 * gw), nxt(3 * gw),
                  row(gw), row(gw), perb(kmt), perb(vm)] + [full(a) for a in consts],
        out_specs=row(D_MODEL),
        out_shape=jax.ShapeDtypeStruct((t, D_MODEL), F32),
        scratch_shapes=[pltpu.VMEM((ts + 2 * HALO, gw), F32)] * 3
        + [pltpu.VMEM((SUBLANES, ts + 2 * HALO - SUBLANES, gw), F32)],
        compiler_params=_params(("parallel",)),
        name="mixers_out_projection",
    )(x2d, upool, upool, upool, uconv, uconv, uconv, usc, usc, usc, umem, ydiff, kmt, vm, *consts)


def _sorting_network(n):
    pairs = []
    p = 1
    while p < n:
        k = p
        while k >= 1:
            for j in range(k % p, n - k, 2 * k):
                for i in range(min(k, n - j - k)):
                    if (i + j) // (2 * p) == (i + j + k) // (2 * p):
                        pairs.append((i + j, i + j + k))
            k //= 2
        p *= 2
    return pairs


def _sort_blocks_descending(blocks):
    blocks = list(blocks)
    for i, j in _sorting_network(len(blocks)):
        hi, lo = jnp.maximum(blocks[i], blocks[j]), jnp.minimum(blocks[i], blocks[j])
        blocks[i], blocks[j] = hi, lo
    return blocks


def _pop_top_values(stack, singles, n):
    stack, singles = list(stack), list(singles)
    vals = []
    for k in range(n):
        top = stack[0]
        for blk in singles:
            top = jnp.maximum(top, blk)
        m = jnp.max(top, axis=0, keepdims=True)
        vals.append(m)
        keep = min(len(stack), n - k - 1)
        if keep == 0:
            break
        hit = stack[0] == m
        stack = [jnp.where(hit, stack[r + 1] if r + 1 < len(stack) else NEG, stack[r]) for r in range(keep)]
        singles = [jnp.where(blk == m, NEG, blk) for blk in singles]
    return vals


def _retrieval_kernel(x_ref, g_ref, wqt_ref, sk_ref, xnt_ref, thr_ref, e1_ref, e2_ref):
    tm = x_ref.shape[0]
    nk = PEER_N_KEYS
    n = PEER_TOPK + 1
    xb = _rms(x_ref[...], g_ref[...]).T.astype(BF16)
    xnt_ref[...] = xb
    qt = jnp.dot(wqt_ref[...], xb, preferred_element_type=F32)
    sub = SUBLANES
    row_id = lax.broadcasted_iota(jnp.int32, (sub, tm), 0)
    for h in range(PEER_HEADS):
        sc = []
        for j in range(2):
            r0 = (2 * h + j) * nk
            sc.append(jnp.dot(sk_ref[2 * h + j], qt[r0:r0 + nk].astype(BF16), preferred_element_type=F32))
        v1, v2 = (_pop_top_values(_sort_blocks_descending([s[r:r + sub] for r in range(0, nk, sub)]), [], n)
                  for s in sc)
        v2_blocks = []
        for r0 in range(0, n, sub):
            blk = jnp.full((sub, tm), NEG, F32)
            for i in range(r0, min(r0 + sub, n)):
                blk = jnp.where(row_id == i - r0, v2[i], blk)
            v2_blocks.append(blk)
        stack = [v1[0] + v2_blocks[0]]
        for i in range(1, n):
            stack.append(v1[i] + jnp.where(row_id < n // (i + 1), v2_blocks[0], NEG))
        best = _pop_top_values(stack, [v1[0] + blk for blk in v2_blocks[1:]], n)
        z = jnp.zeros((1, tm), F32)
        for i in range(PEER_TOPK):
            z = z + jnp.exp(best[i] - best[0])
        half_inv_z = math.sqrt(0.5) / z
        tau = 0.5 * (best[PEER_TOPK - 1] + best[PEER_TOPK])
        thr = jnp.exp(tau - sc[0] - v2[0]) * half_inv_z
        e1 = jnp.exp(sc[0] - v1[0])
        e2 = jnp.exp(sc[1] - v2[0]) * half_inv_z
        for c in range(tm // LANES):
            cs = slice(c * LANES, (c + 1) * LANES)
            thr_ref[h, c] = thr[:, cs]
            e1_ref[h, c] = e1[:, cs]
            e2_ref[h, c] = e2[:, cs]


def _retrieval(x2d, g, wqt, sk):
    t = x2d.shape[0]
    tm = min(TM_RET, t)
    nk = PEER_N_KEYS
    full = lambda a: pl.BlockSpec(a.shape, lambda i: (0,) * a.ndim)
    tab = pl.BlockSpec((PEER_HEADS, tm // LANES, nk, LANES), lambda i: (0, i, 0, 0))
    tab_shape = jax.ShapeDtypeStruct((PEER_HEADS, t // LANES, nk, LANES), F32)
    return pl.pallas_call(
        _retrieval_kernel,
        grid=(t // tm,),
        in_specs=[pl.BlockSpec((tm, D_MODEL), lambda i: (i, 0)), full(g), full(wqt), full(sk)],
        out_specs=[pl.BlockSpec((D_MODEL, tm), lambda i: (0, i)), tab, tab, tab],
        out_shape=[jax.ShapeDtypeStruct((D_MODEL, t), BF16), tab_shape, tab_shape, tab_shape],
        compiler_params=_params(("parallel",)),
        name="peer_retrieval",
    )(x2d, g, wqt, sk)


def _expert_kernel(x_ref, xnt_ref, u_ref, vt_ref, thr_ref, e1_ref, e2_ref, o_ref, acc_ref, w_ref):
    nk = PEER_N_KEYS
    eb = pl.program_id(1)

    @pl.when(eb == 0)
    def _():
        acc_ref[...] = jnp.zeros_like(acc_ref)

    tm = xnt_ref.shape[1]
    ht = jnp.dot(u_ref[0], xnt_ref[...], preferred_element_type=F32)
    for al in range(A_BLOCK):
        for c in range(tm // LANES):
            cs = slice(c * LANES, (c + 1) * LANES)
            hs = ht[al * nk:(al + 1) * nk, cs]
            gate = None
            for h in range(PEER_HEADS):
                e2 = e2_ref[h, c]
                sel = jnp.where(e2 >= thr_ref[h, c, al:al + 1, :], e2, 0.0)
                term = e1_ref[h, c, al:al + 1, :] * sel
                gate = term if gate is None else gate + term
            w_ref[al * nk:(al + 1) * nk, cs] = (gate * (hs * (1.0 + lax.erf(hs)))).astype(BF16)
    acc_ref[...] += jnp.dot(vt_ref[0, 0], w_ref[...], preferred_element_type=F32)

    @pl.when(eb == pl.num_programs(1) - 1)
    def _():
        o_ref[...] = x_ref[...] + acc_ref[...].T


def _experts(x2d, xnt, u, vt, thr, e1, e2, layer):
    t = x2d.shape[0]
    tm = min(TM_EXP, t)
    nk = PEER_N_KEYS
    te = A_BLOCK * nk
    nc = tm // LANES
    tab_a = pl.BlockSpec((PEER_HEADS, nc, A_BLOCK, LANES), lambda i, e: (0, i, e, 0))
    tab_b = pl.BlockSpec((PEER_HEADS, nc, nk, LANES), lambda i, e: (0, i, 0, 0))
    return pl.pallas_call(
        _expert_kernel,
        grid=(t // tm, nk // A_BLOCK),
        in_specs=[pl.BlockSpec((tm, D_MODEL), lambda i, e: (i, 0)),
                  pl.BlockSpec((D_MODEL, tm), lambda i, e: (0, i)),
                  pl.BlockSpec((1, te, D_MODEL), lambda i, e: (layer, e, 0)),
                  pl.BlockSpec((1, 1, D_MODEL, te), lambda i, e: (layer, e, 0, 0)),
                  tab_a, tab_a, tab_b],
        out_specs=pl.BlockSpec((tm, D_MODEL), lambda i, e: (i, 0)),
        out_shape=jax.ShapeDtypeStruct((t, D_MODEL), F32),
        scratch_shapes=[pltpu.VMEM((D_MODEL, tm), F32), pltpu.VMEM((te, tm), BF16)],
        compiler_params=_params(("parallel", "arbitrary")),
        name="peer_experts",
    )(x2d, xnt, u, vt, thr, e1, e2)


def _segment_mean_matrix(width, seg):
    idx = jnp.arange(width) // seg
    return (idx[:, None] == idx[None, :]).astype(F32) / seg


def _block_diag(w):
    g, c, _ = w.shape
    eye = jnp.eye(g, dtype=w.dtype)
    return (eye[:, None, :, None] * w[:, :, None, :]).reshape(g * c, g * c)


def kernel(x, mem, norm_mix, w_in, pool_w, pool_scale, conv_dw, conv_ln_g, conv_ln_b, conv_pw, diff_qk_norm,
           diff_lambda, diff_subln, sconv_w, mem_norm, w_mem_kv, mem_qk_norm, group_norm, w_out, norm_ffn,
           peer_wq, peer_subkeys, peer_u, peer_v):
    b, s, d = x.shape
    depth = w_in.shape[0]
    gw = GROUP_WIDTH
    seg32 = _segment_mean_matrix(gw, DIFF_HEAD_DIM)
    seg64 = _segment_mean_matrix(gw, MEM_HEAD_DIM)
    row = lambda a: a.reshape(1, -1)
    tile = lambda a, n: jnp.tile(a, n).reshape(1, -1)
    x2d = x.reshape(b * s, d)
    u_all = (peer_u * math.sqrt(0.5)).astype(BF16)
    vt_all = peer_v.astype(BF16).reshape(depth, -1, A_BLOCK * PEER_N_KEYS, d).transpose(0, 1, 3, 2)
    for l in range(depth):
        lam_init = jnp.full((1, 1), 0.8 - 0.6 * math.exp(-0.3 * l), F32)
        upool, uconv, usc, umem, q, kt, v = _inproj(
            x2d, row(norm_mix[l]), w_in[l].astype(BF16), tile(diff_qk_norm[l, 0], 2 * DIFF_HEADS),
            tile(diff_qk_norm[l, 1], 2 * DIFF_HEADS), seg32, b, s)
        kmt, vm = _memkv(mem, row(mem_norm[l]), w_mem_kv[l].astype(BF16), tile(mem_qk_norm[l, 1], MEM_HEADS), seg64)
        ydiff = _diffattn(q, kt, v, diff_lambda[l], lam_init, tile(diff_subln[l], DIFF_HEADS), seg64, b, s)
        x2d = _mixers(x2d, upool, uconv, usc, umem, ydiff, kmt, vm,
                      _block_diag(pool_w[l]).astype(BF16), row(pool_scale[l]), conv_dw[l], row(conv_ln_g[l]),
                      row(conv_ln_b[l]), conv_pw[l].astype(BF16), sconv_w[l], tile(mem_qk_norm[l, 0], MEM_HEADS),
                      seg64, group_norm[l].reshape(N_GROUPS, gw), w_out[l].astype(BF16), b, s)
        xnt, thr, e1, e2 = _retrieval(
            x2d, row(norm_ffn[l]), peer_wq[l].T.astype(BF16),
            peer_subkeys[l].reshape(2 * PEER_HEADS, PEER_N_KEYS, -1).astype(BF16))
        x2d = _experts(x2d, xnt, u_all, vt_all, thr, e1, e2, l)
    return x2d.reshape(b, s, d)
```

```python
import functools
import math

import jax
import jax.numpy as jnp
import numpy as np
from jax import lax
from jax.experimental import pallas as pl
from jax.experimental.pallas import tpu as pltpu

F32 = jnp.float32
BF16 = jnp.bfloat16

D_MODEL = 1024
GROUP_WIDTH = 256
N_GROUPS = 5
POOL_HALF_WINDOWS = (1, 2, 4, 8)
POOL_CH = 64
CONV_WIDTH = 31
SHORT_CONV_WIDTH = 3
DIFF_HEADS = 4
DIFF_HEAD_DIM = 32
MEM_HEADS = 4
MEM_HEAD_DIM = 64
PEER_HEADS = 8
PEER_N_KEYS = 128
PEER_TOPK = 16
EPS = 1e-6

LANES = 128
SUBLANES = 8
HALO = 16
VMEM_LIMIT = 56 * 1024 * 1024

TM_IN = 512
TQ_ATT = 256
TS_MIX = 512
TM_RET = 256
TM_EXP = 1024
A_BLOCK = 8
NEG = -1e30
LOG2E = 1.4426950408889634


def _params(sem):
    return pltpu.CompilerParams(dimension_semantics=sem, vmem_limit_bytes=VMEM_LIMIT)


def _rms(x, g):
    return x * lax.rsqrt(jnp.mean(x * x, axis=-1, keepdims=True) + EPS) * g


def _seg_rms(x, g, seg_mean):
    ms = jnp.dot(x * x, seg_mean, precision=lax.Precision.HIGHEST, preferred_element_type=F32)
    return x * lax.rsqrt(ms + EPS) * g


def _lane_range_mask(shape, lo, hi):
    lane = lax.broadcasted_iota(jnp.int32, shape, len(shape) - 1)
    return (lane >= lo) & (lane < hi)


def _inproj_kernel(x_ref, g_ref, w_ref, gq_ref, gk_ref, seg_ref,
                   upool_ref, uconv_ref, usc_ref, umem_ref, q_ref, kt_ref, v_ref):
    xb = _rms(x_ref[...], g_ref[...]).astype(BF16)

    def proj(lo, hi):
        return jnp.dot(xb, w_ref[:, lo:hi], preferred_element_type=F32)

    gw = GROUP_WIDTH
    upool_ref[...] = proj(0, gw)
    uconv_ref[...] = proj(gw, 3 * gw)
    q = proj(3 * gw, 4 * gw)
    k = proj(4 * gw, 5 * gw)
    v_ref[...] = proj(5 * gw, 6 * gw).astype(BF16)
    usc_ref[...] = proj(6 * gw, 9 * gw)
    umem_ref[...] = proj(9 * gw, 10 * gw)
    seg = seg_ref[...]
    q_ref[...] = (_seg_rms(q, gq_ref[...], seg) * (DIFF_HEAD_DIM ** -0.5 * LOG2E)).astype(BF16)
    kt_ref[0] = _seg_rms(k, gk_ref[...], seg).T.astype(BF16)


def _inproj(x2d, g, w, gq, gk, seg32, batch, seq):
    t = x2d.shape[0]
    tm = min(TM_IN, seq)
    nst = seq // tm
    gw = GROUP_WIDTH
    row = lambda c: pl.BlockSpec((tm, c), lambda i: (i, 0))
    full = lambda a: pl.BlockSpec(a.shape, lambda i: (0,) * a.ndim)
    return pl.pallas_call(
        _inproj_kernel,
        grid=(t // tm,),
        in_specs=[row(D_MODEL), full(g), full(w), full(gq), full(gk), full(seg32)],
        out_specs=[row(gw), row(2 * gw), row(3 * gw), row(gw), row(gw),
                   pl.BlockSpec((1, gw, tm), lambda i: (i // nst, 0, i % nst)), row(gw)],
        out_shape=[jax.ShapeDtypeStruct((t, gw), F32), jax.ShapeDtypeStruct((t, 2 * gw), F32),
                   jax.ShapeDtypeStruct((t, 3 * gw), F32), jax.ShapeDtypeStruct((t, gw), F32),
                   jax.ShapeDtypeStruct((t, gw), BF16), jax.ShapeDtypeStruct((batch, gw, seq), BF16),
                   jax.ShapeDtypeStruct((t, gw), BF16)],
        compiler_params=_params(("parallel",)),
        name="in_projection",
    )(x2d, g, w, gq, gk, seg32)


def _memkv_kernel(mem_ref, g_ref, w_ref, gk_ref, seg_ref, kt_ref, v_ref):
    mn = _rms(mem_ref[0], g_ref[...]).astype(BF16)
    kv = jnp.dot(mn, w_ref[...], preferred_element_type=F32)
    k = _seg_rms(kv[:, :GROUP_WIDTH], gk_ref[...], seg_ref[...])
    kt_ref[0] = k.T.astype(BF16)
    v_ref[0] = kv[:, GROUP_WIDTH:].astype(BF16)


def _memkv(mem, g, w, gk, seg64):
    b, m, _ = mem.shape
    gw = GROUP_WIDTH
    full = lambda a: pl.BlockSpec(a.shape, lambda i: (0,) * a.ndim)
    return pl.pallas_call(
        _memkv_kernel,
        grid=(b,),
        in_specs=[pl.BlockSpec((1, m, D_MODEL), lambda i: (i, 0, 0)), full(g), full(w), full(gk), full(seg64)],
        out_specs=[pl.BlockSpec((1, gw, m), lambda i: (i, 0, 0)), pl.BlockSpec((1, m, gw), lambda i: (i, 0, 0))],
        out_shape=[jax.ShapeDtypeStruct((b, gw, m), BF16), jax.ShapeDtypeStruct((b, m, gw), BF16)],
        compiler_params=_params(("parallel",)),
        name="memory_kv",
    )(mem, g, w, gk, seg64)


def _bf16_parts(x, n):
    parts = []
    for _ in range(n):
        bits = np.array([x], np.float32).view(np.uint32)[0]
        bits = (bits + (((bits >> 16) & 1) + 0x7FFF)) & 0xFFFF0000
        p = float(np.array([bits], np.uint32).view(np.float32)[0])
        parts.append(p)
        x -= p
    return parts


def _by_index(idx, values):
    out = jnp.full(idx.shape, values[-1], F32)
    for i in range(len(values) - 2, -1, -1):
        out = jnp.where(idx == i, values[i], out)
    return out


POS_SPLIT = 64
N_LOG2E_PARTS = 4


def _diffattn_kernel(q_ref, kt_ref, v_ref, lam_ref, laminit_ref, gsub_ref, seg_ref, o_ref, s_ref):
    tq = q_ref.shape[0]
    seq = kt_ref.shape[2]
    lp = lam_ref[...]
    lam_init = laminit_ref[...]
    lam = (jnp.exp(jnp.sum(lp[0:1] * lp[1:2], axis=-1, keepdims=True))
           - jnp.exp(jnp.sum(lp[2:3] * lp[3:4], axis=-1, keepdims=True)) + lam_init)
    q0 = pl.multiple_of(pl.program_id(1) * tq, tq)
    q = q_ref[...]
    kt = kt_ref[0]
    v = v_ref[...]
    half = DIFF_HEADS * DIFF_HEAD_DIM
    npart = N_LOG2E_PARTS
    log2e = _bf16_parts(LOG2E, npart)
    arow = lax.broadcasted_iota(jnp.int32, (4 * npart, seq), 0)
    kpos = lax.broadcasted_iota(jnp.int32, (4 * npart, seq), 1)
    sign = jnp.where(kpos < q0, 1.0, jnp.where(kpos >= q0 + tq, -1.0, 0.0))
    k_hi = (kpos & -POS_SPLIT).astype(F32)
    k_lo = (kpos & (POS_SPLIT - 1)).astype(F32)
    k_aug = sign * jnp.where(arow < npart, k_hi, jnp.where(arow < 2 * npart, k_lo, _by_index(arow % npart, log2e)))
    k_aug = jnp.concatenate([k_aug.astype(BF16), jnp.zeros((half - 4 * npart, seq), BF16)], axis=0)
    alane = lax.broadcasted_iota(jnp.int32, (tq, half), 1)
    qpos = q0 + lax.broadcasted_iota(jnp.int32, (tq, half), 0)
    q_hi = (qpos & -POS_SPLIT).astype(F32)
    q_lo = (qpos & (POS_SPLIT - 1)).astype(F32)
    q_aug = jnp.where(alane < 2 * npart, _by_index(alane % npart, log2e),
                      jnp.where(alane < 3 * npart, -q_hi, jnp.where(alane < 4 * npart, -q_lo, 0.0)))
    local = jnp.abs(lax.broadcasted_iota(jnp.int32, (tq, tq), 0)
                    - lax.broadcasted_iota(jnp.int32, (tq, tq), 1)).astype(F32)
    o = jnp.zeros((tq, GROUP_WIDTH), F32)
    for h in range(DIFF_HEADS):
        slope = 2.0 ** (-8.0 * (h + 1) / DIFF_HEADS)
        q_aug_h = (slope * q_aug).astype(BF16)
        own_bias = (slope * LOG2E) * local
        vlo = h * 2 * DIFF_HEAD_DIM
        zlane = (vlo + 2 * DIFF_HEAD_DIM) % GROUP_WIDTH
        v_ones = jnp.where(_lane_range_mask(v.shape, zlane, zlane + 1), jnp.ones_like(v), v)
        outs = []
        for j in range(2):
            qj = q[:, j * half:(j + 1) * half]
            lo = h * DIFF_HEAD_DIM
            qm = jnp.where(_lane_range_mask(qj.shape, lo, lo + DIFF_HEAD_DIM), qj, jnp.zeros_like(qj))
            s_ref[j] = jnp.dot(jnp.concatenate([qm, q_aug_h], axis=1),
                               jnp.concatenate([kt[j * half:(j + 1) * half], k_aug], axis=0),
                               preferred_element_type=F32)
            s_ref[j, :, pl.ds(q0, tq)] = s_ref[j, :, pl.ds(q0, tq)] - own_bias
            s = s_ref[j]
            e = jnp.exp2(s - jnp.max(s, axis=-1, keepdims=True))
            ev = jnp.dot(e.astype(BF16), v_ones, preferred_element_type=F32)
            outs.append(ev * (1.0 / ev[:, zlane:zlane + 1]))
        oh = outs[0] - lam * outs[1]
        o = jnp.where(_lane_range_mask(o.shape, vlo, vlo + 2 * DIFF_HEAD_DIM), oh, o)
    o_ref[...] = _seg_rms(o, gsub_ref[...], seg_ref[...]) * (1.0 - lam_init)


def _diffattn(q, kt, v, lam_p, lam_init, gsub, seg64, batch, seq):
    t = q.shape[0]
    gw = GROUP_WIDTH
    tq = min(TQ_ATT, seq)
    nq = seq // tq
    full = lambda a: pl.BlockSpec(a.shape, lambda b, i: (0,) * a.ndim)
    return pl.pallas_call(
        _diffattn_kernel,
        grid=(batch, nq),
        in_specs=[pl.BlockSpec((tq, gw), lambda b, i: (b * nq + i, 0)),
                  pl.BlockSpec((1, gw, seq), lambda b, i: (b, 0, 0)),
                  pl.BlockSpec((seq, gw), lambda b, i: (b, 0)),
                  full(lam_p), full(lam_init), full(gsub), full(seg64)],
        out_specs=pl.BlockSpec((tq, gw), lambda b, i: (b * nq + i, 0)),
        out_shape=jax.ShapeDtypeStruct((t, gw), F32),
        scratch_shapes=[pltpu.VMEM((2, tq, seq), F32)],
        compiler_params=_params(("parallel", "parallel")),
        name="diff_attention",
    )(q, kt, v, lam_p, lam_init, gsub, seg64)


def _mixer_kernel(x_ref, up_ref, up_prev, up_next, uc_ref, uc_prev, uc_next, us_ref, us_prev, us_next,
                  um_ref, yd_ref, kmt_ref, vm_ref,
                  poolw_ref, pools_ref, dw_ref, lng_ref, lnb_ref, pw_ref, sw_ref, gqm_ref, seg_ref,
                  gn_ref, wout_ref, o_ref, ext_pool, ext_conv, ext_sc, shift_conv, *, seq):
    ts = x_ref.shape[0]
    gw = GROUP_WIDTH
    nst = seq // ts
    it = pl.program_id(0) % nst
    keep_prev = jnp.where(it > 0, 1.0, 0.0)
    keep_next = jnp.where(it < nst - 1, 1.0, 0.0)

    def fill(ext, prev, main, nxt):
        ext[0:HALO, :] = prev * keep_prev
        ext[HALO:HALO + ts, :] = main
        ext[HALO + ts:HALO + ts + HALO, :] = nxt * keep_next

    def shifted(ext, k):
        return ext[HALO + k:HALO + k + ts, :]

    fill(ext_pool, up_prev[...], up_ref[...], up_next[...])
    u = up_ref[...]
    lane = lax.broadcasted_iota(jnp.int32, (ts, gw), 1)
    pos = it * ts + lax.broadcasted_iota(jnp.int32, (ts, gw), 0)
    wsum = jnp.zeros((ts, gw), F32)
    halfw = jnp.zeros((ts, gw), jnp.int32)
    run = None
    prev_half = 0
    for gi, hw in enumerate(POOL_HALF_WINDOWS):
        for k in list(range(-hw, -prev_half)) + list(range(prev_half, hw)):
            term = shifted(ext_pool, k)
            run = term if run is None else run + term
        prev_half = hw
        in_group = (lane >= gi * POOL_CH) & (lane < (gi + 1) * POOL_CH)
        wsum = jnp.where(in_group, run, wsum)
        halfw = jnp.where(in_group, hw, halfw)
    cnt = (jnp.minimum(pos + halfw, seq) - jnp.maximum(pos - halfw, 0)).astype(F32)
    pooled = wsum / cnt - u
    y_pool = jnp.dot(pooled.astype(BF16), poolw_ref[...], preferred_element_type=F32) * pools_ref[...]

    def glu(ucv):
        return ucv[:, :gw] * jax.nn.sigmoid(ucv[:, gw:])

    fill(ext_conv, glu(uc_prev[...]), glu(uc_ref[...]), glu(uc_next[...]))
    pad = CONV_WIDTH // 2
    for r in range(SUBLANES):
        shift_conv[r] = ext_conv[r:r + shift_conv.shape[1], :]
    conv = jnp.zeros((ts, gw), F32)
    for k in range(CONV_WIDTH):
        m, r = divmod(HALO - pad + k, SUBLANES)
        conv = conv + shift_conv[r, SUBLANES * m:SUBLANES * m + ts, :] * dw_ref[k:k + 1, :]
    mu = jnp.mean(conv, axis=-1, keepdims=True)
    cen = conv - mu
    var = jnp.mean(cen * cen, axis=-1, keepdims=True)
    hln = cen * lax.rsqrt(var + EPS) * lng_ref[...] + lnb_ref[...]
    hact = hln * jax.nn.sigmoid(hln)
    y_conv = jnp.dot(hact.astype(BF16), pw_ref[...], preferred_element_type=F32)

    def ch(usv):
        return usv[:, 2 * gw:] * usv[:, :gw]

    fill(ext_sc, ch(us_prev[...]), ch(us_ref[...]), ch(us_next[...]))
    spad = SHORT_CONV_WIDTH // 2
    sconv = jnp.zeros((ts, gw), F32)
    for k in range(SHORT_CONV_WIDTH):
        sconv = sconv + shifted(ext_sc, k - spad) * sw_ref[k:k + 1, :]
    y_sc = us_ref[:, gw:2 * gw] * sconv

    qn = (_seg_rms(um_ref[...], gqm_ref[...], seg_ref[...]) * (MEM_HEAD_DIM ** -0.5)).astype(BF16)
    kmt = kmt_ref[0]
    vm = vm_ref[0]
    y_mem = jnp.zeros((ts, gw), F32)
    for h in range(MEM_HEADS):
        hmask = _lane_range_mask((ts, gw), h * MEM_HEAD_DIM, (h + 1) * MEM_HEAD_DIM)
        qm = jnp.where(hmask, qn, jnp.zeros_like(qn))
        s = jnp.dot(qm, kmt, preferred_element_type=F32)
        e = jnp.exp(s - jnp.max(s, axis=-1, keepdims=True))
        p = (e * (1.0 / jnp.sum(e, axis=-1, keepdims=True))).astype(BF16)
        y_mem = jnp.where(hmask, jnp.dot(p, vm, preferred_element_type=F32), y_mem)

    acc = x_ref[...]
    for gi, y in enumerate((y_pool, y_conv, yd_ref[...], y_sc, y_mem)):
        yn = _rms(y, gn_ref[gi:gi + 1, :]).astype(BF16)
        acc = acc + jnp.dot(yn, wout_ref[gi * gw:(gi + 1) * gw, :], preferred_element_type=F32)
    o_ref[...] = acc


def _mixers(x2d, upool, uconv, usc, umem, ydiff, kmt, vm, poolw, pools, dw, lng, lnb, pw, sw, gqm, seg64,
            gn, wout, batch, seq):
    t = x2d.shape[0]
    gw = GROUP_WIDTH
    ts = min(TS_MIX, seq)
    nst = seq // ts
    r = ts // HALO
    last = t // HALO - 1
    row = lambda c: pl.BlockSpec((ts, c), lambda i: (i, 0))
    prev = lambda c: pl.BlockSpec((HALO, c), lambda i: (jnp.maximum(i * r - 1, 0), 0))
    nxt = lambda c: pl.BlockSpec((HALO, c), lambda i: (jnp.minimum((i + 1) * r, last), 0))
    full = lambda a: pl.BlockSpec(a.shape, lambda i: (0,) * a.ndim)
    perb = lambda a: pl.BlockSpec((1,) + a.shape[1:], lambda i: (i // nst, 0, 0))
    consts = (poolw, pools, dw, lng, lnb, pw, sw, gqm, seg64, gn, wout)
    return pl.pallas_call(
        functools.partial(_mixer_kernel, seq=seq),
        grid=(t // ts,),
        in_specs=[row(D_MODEL),
                  row(gw), prev(gw), nxt(gw),
                  row(2 * gw), prev(2 * gw), nxt(2 * gw),
                  row(3 * gw), prev(3 * gw), nxt(3 * gw),
                  row(gw), row(gw), perb(kmt), perb(vm)] + [full(a) for a in consts],
        out_specs=row(D_MODEL),
        out_shape=jax.ShapeDtypeStruct((t, D_MODEL), F32),
        scratch_shapes=[pltpu.VMEM((ts + 2 * HALO, gw), F32)] * 3
        + [pltpu.VMEM((SUBLANES, ts + 2 * HALO - SUBLANES, gw), F32)],
        compiler_params=_params(("parallel",)),
        name="mixers_out_projection",
    )(x2d, upool, upool, upool, uconv, uconv, uconv, usc, usc, usc, umem, ydiff, kmt, vm, *consts)


def _sorting_network(n):
    pairs = []
    p = 1
    while p < n:
        k = p
        while k >= 1:
            for j in range(k % p, n - k, 2 * k):
                for i in range(min(k, n - j - k)):
                    if (i + j) // (2 * p) == (i + j + k) // (2 * p):
                        pairs.append((i + j, i + j + k))
            k //= 2
        p *= 2
    return pairs


def _sort_blocks_descending(blocks):
    blocks = list(blocks)
    for i, j in _sorting_network(len(blocks)):
        hi, lo = jnp.maximum(blocks[i], blocks[j]), jnp.minimum(blocks[i], blocks[j])
        blocks[i], blocks[j] = hi, lo
    return blocks


def _pop_top_values(stack, singles, n):
    stack, singles = list(stack), list(singles)
    vals = []
    for k in range(n):
        top = stack[0]
        for blk in singles:
            top = jnp.maximum(top, blk)
        m = jnp.max(top, axis=0, keepdims=True)
        vals.append(m)
        keep = min(len(stack), n - k - 1)
        if keep == 0:
            break
        hit = stack[0] == m
        stack = [jnp.where(hit, stack[r + 1] if r + 1 < len(stack) else NEG, stack[r]) for r in range(keep)]
        singles = [jnp.where(blk == m, NEG, blk) for blk in singles]
    return vals


def _retrieval_kernel(x_ref, g_ref, wqt_ref, sk_ref, xnt_ref, thr_ref, e1_ref, e2_ref):
    tm = x_ref.shape[0]
    nk = PEER_N_KEYS
    n = PEER_TOPK + 1
    xb = _rms(x_ref[...], g_ref[...]).T.astype(BF16)
    xnt_ref[...] = xb
    qt = jnp.dot(wqt_ref[...], xb, preferred_element_type=F32)
    sub = SUBLANES
    row_id = lax.broadcasted_iota(jnp.int32, (sub, tm), 0)
    for h in range(PEER_HEADS):
        sc = []
        for j in range(2):
            r0 = (2 * h + j) * nk
            sc.append(jnp.dot(sk_ref[2 * h + j], qt[r0:r0 + nk].astype(BF16), preferred_element_type=F32))
        v1, v2 = (_pop_top_values(_sort_blocks_descending([s[r:r + sub] for r in range(0, nk, sub)]), [], n)
                  for s in sc)
        v2_blocks = []
        for r0 in range(0, n, sub):
            blk = jnp.full((sub, tm), NEG, F32)
            for i in range(r0, min(r0 + sub, n)):
                blk = jnp.where(row_id == i - r0, v2[i], blk)
            v2_blocks.append(blk)
        stack = [v1[0] + v2_blocks[0]]
        for i in range(1, n):
            stack.append(v1[i] + jnp.where(row_id < n // (i + 1), v2_blocks[0], NEG))
        best = _pop_top_values(stack, [v1[0] + blk for blk in v2_blocks[1:]], n)
        z = jnp.zeros((1, tm), F32)
        for i in range(PEER_TOPK):
            z = z + jnp.exp(best[i] - best[0])
        half_inv_z = math.sqrt(0.5) / z
        tau = 0.5 * (best[PEER_TOPK - 1] + best[PEER_TOPK])
        thr = jnp.exp(tau - sc[0] - v2[0]) * half_inv_z
        e1 = jnp.exp(sc[0] - v1[0])
        e2 = jnp.exp(sc[1] - v2[0]) * half_inv_z
        for c in range(tm // LANES):
            cs = slice(c * LANES, (c + 1) * LANES)
            thr_ref[h, c] = thr[:, cs]
            e1_ref[h, c] = e1[:, cs]
            e2_ref[h, c] = e2[:, cs]


def _retrieval(x2d, g, wqt, sk):
    t = x2d.shape[0]
    tm = min(TM_RET, t)
    nk = PEER_N_KEYS
    full = lambda a: pl.BlockSpec(a.shape, lambda i: (0,) * a.ndim)
    tab = pl.BlockSpec((PEER_HEADS, tm // LANES, nk, LANES), lambda i: (0, i, 0, 0))
    tab_shape = jax.ShapeDtypeStruct((PEER_HEADS, t // LANES, nk, LANES), F32)
    return pl.pallas_call(
        _retrieval_kernel,
        grid=(t // tm,),
        in_specs=[pl.BlockSpec((tm, D_MODEL), lambda i: (i, 0)), full(g), full(wqt), full(sk)],
        out_specs=[pl.BlockSpec((D_MODEL, tm), lambda i: (0, i)), tab, tab, tab],
        out_shape=[jax.ShapeDtypeStruct((D_MODEL, t), BF16), tab_shape, tab_shape, tab_shape],
        compiler_params=_params(("parallel",)),
        name="peer_retrieval",
    )(x2d, g, wqt, sk)


def _expert_kernel(x_ref, xnt_ref, u_ref, vt_ref, thr_ref, e1_ref, e2_ref, o_ref, acc_ref, w_ref):
    nk = PEER_N_KEYS
    eb = pl.program_id(1)

    @pl.when(eb == 0)
    def _():
        acc_ref[...] = jnp.zeros_like(acc_ref)

    tm = xnt_ref.shape[1]
    ht = jnp.dot(u_ref[0], xnt_ref[...], preferred_element_type=F32)
    for al in range(A_BLOCK):
        for c in range(tm // LANES):
            cs = slice(c * LANES, (c + 1) * LANES)
            hs = ht[al * nk:(al + 1) * nk, cs]
            gate = None
            for h in range(PEER_HEADS):
                e2 = e2_ref[h, c]
                sel = jnp.where(e2 >= thr_ref[h, c, al:al + 1, :], e2, 0.0).astype(BF16)
                term = e1_ref[h, c, al:al + 1, :].astype(BF16) * sel
                gate = term if gate is None else gate + term
            w_ref[al * nk:(al + 1) * nk, cs] = gate * (hs * (1.0 + lax.erf(hs))).astype(BF16)
    acc_ref[...] += jnp.dot(vt_ref[0, 0], w_ref[...], preferred_element_type=F32)

    @pl.when(eb == pl.num_programs(1) - 1)
    def _():
        o_ref[...] = x_ref[...] + acc_ref[...].T


def _experts(x2d, xnt, u, vt, thr, e1, e2, layer):
    t = x2d.shape[0]
    tm = min(TM_EXP, t)
    nk = PEER_N_KEYS
    te = A_BLOCK * nk
    nc = tm // LANES
    tab_a = pl.BlockSpec((PEER_HEADS, nc, A_BLOCK, LANES), lambda i, e: (0, i, e, 0))
    tab_b = pl.BlockSpec((PEER_HEADS, nc, nk, LANES), lambda i, e: (0, i, 0, 0))
    return pl.pallas_call(
        _expert_kernel,
        grid=(t // tm, nk // A_BLOCK),
        in_specs=[pl.BlockSpec((tm, D_MODEL), lambda i, e: (i, 0)),
                  pl.BlockSpec((D_MODEL, tm), lambda i, e: (0, i)),
                  pl.BlockSpec((1, te, D_MODEL), lambda i, e: (layer, e, 0)),
                  pl.BlockSpec((1, 1, D_MODEL, te), lambda i, e: (layer, e, 0, 0)),
                  tab_a, tab_a, tab_b],
        out_specs=pl.BlockSpec((tm, D_MODEL), lambda i, e: (i, 0)),
        out_shape=jax.ShapeDtypeStruct((t, D_MODEL), F32),
        scratch_shapes=[pltpu.VMEM((D_MODEL, tm), F32), pltpu.VMEM((te, tm), BF16)],
        compiler_params=_params(("parallel", "arbitrary")),
        name="peer_experts",
    )(x2d, xnt, u, vt, thr, e1, e2)


def _segment_mean_matrix(width, seg):
    idx = jnp.arange(width) // seg
    return (idx[:, None] == idx[None, :]).astype(F32) / seg


def _block_diag(w):
    g, c, _ = w.shape
    eye = jnp.eye(g, dtype=w.dtype)
    return (eye[:, None, :, None] * w[:, :, None, :]).reshape(g * c, g * c)


def kernel(x, mem, norm_mix, w_in, pool_w, pool_scale, conv_dw, conv_ln_g, conv_ln_b, conv_pw, diff_qk_norm,
           diff_lambda, diff_subln, sconv_w, mem_norm, w_mem_kv, mem_qk_norm, group_norm, w_out, norm_ffn,
           peer_wq, peer_subkeys, peer_u, peer_v):
    b, s, d = x.shape
    depth = w_in.shape[0]
    gw = GROUP_WIDTH
    seg32 = _segment_mean_matrix(gw, DIFF_HEAD_DIM)
    seg64 = _segment_mean_matrix(gw, MEM_HEAD_DIM)
    row = lambda a: a.reshape(1, -1)
    tile = lambda a, n: jnp.tile(a, n).reshape(1, -1)
    x2d = x.reshape(b * s, d)
    u_all = (peer_u * math.sqrt(0.5)).astype(BF16)
    vt_all = peer_v.astype(BF16).reshape(depth, -1, A_BLOCK * PEER_N_KEYS, d).transpose(0, 1, 3, 2)
    for l in range(depth):
        lam_init = jnp.full((1, 1), 0.8 - 0.6 * math.exp(-0.3 * l), F32)
        upool, uconv, usc, umem, q, kt, v = _inproj(
            x2d, row(norm_mix[l]), w_in[l].astype(BF16), tile(diff_qk_norm[l, 0], 2 * DIFF_HEADS),
            tile(diff_qk_norm[l, 1], 2 * DIFF_HEADS), seg32, b, s)
        kmt, vm = _memkv(mem, row(mem_norm[l]), w_mem_kv[l].astype(BF16), tile(mem_qk_norm[l, 1], MEM_HEADS), seg64)
        ydiff = _diffattn(q, kt, v, diff_lambda[l], lam_init, tile(diff_subln[l], DIFF_HEADS), seg64, b, s)
        x2d = _mixers(x2d, upool, uconv, usc, umem, ydiff, kmt, vm,
                      _block_diag(pool_w[l]).astype(BF16), row(pool_scale[l]), conv_dw[l], row(conv_ln_g[l]),
                      row(conv_ln_b[l]), conv_pw[l].astype(BF16), sconv_w[l], tile(mem_qk_norm[l, 0], MEM_HEADS),
                      seg64, group_norm[l].reshape(N_GROUPS, gw), w_out[l].astype(BF16), b, s)
        xnt, thr, e1, e2 = _retrieval(
            x2d, row(norm_ffn[l]), peer_wq[l].T.astype(BF16),
            peer_subkeys[l].reshape(2 * PEER_HEADS, PEER_N_KEYS, -1).astype(BF16))
        x2d = _experts(x2d, xnt, u_all, vt_all, thr, e1, e2, l)
    return x2d.reshape(b, s, d)
```

```python
import functools
import math

import jax
import jax.numpy as jnp
import numpy as np
from jax import lax
from jax.experimental import pallas as pl
from jax.experimental.pallas import tpu as pltpu

F32 = jnp.float32
BF16 = jnp.bfloat16

D_MODEL = 1024
GROUP_WIDTH = 256
N_GROUPS = 5
POOL_HALF_WINDOWS = (1, 2, 4, 8)
POOL_CH = 64
CONV_WIDTH = 31
SHORT_CONV_WIDTH = 3
DIFF_HEADS = 4
DIFF_HEAD_DIM = 32
MEM_HEADS = 4
MEM_HEAD_DIM = 64
PEER_HEADS = 8
PEER_N_KEYS = 128
PEER_TOPK = 16
EPS = 1e-6

LANES = 128
SUBLANES = 8
HALO = 16
VMEM_LIMIT = 56 * 1024 * 1024

TM_IN = 512
TQ_ATT = 256
TS_MIX = 512
TM_RET = 256
TM_EXP = 1024
A_BLOCK = 8
NEG = -1e30
LOG2E = 1.4426950408889634


def _params(sem):
    return pltpu.CompilerParams(dimension_semantics=sem, vmem_limit_bytes=VMEM_LIMIT)


def _rms(x, g):
    return x * lax.rsqrt(jnp.mean(x * x, axis=-1, keepdims=True) + EPS) * g


def _seg_rms(x, g, seg_mean):
    ms = jnp.dot(x * x, seg_mean, precision=lax.Precision.HIGHEST, preferred_element_type=F32)
    return x * lax.rsqrt(ms + EPS) * g


def _lane_range_mask(shape, lo, hi):
    lane = lax.broadcasted_iota(jnp.int32, shape, len(shape) - 1)
    return (lane >= lo) & (lane < hi)


def _inproj_kernel(x_ref, g_ref, w_ref, gq_ref, gk_ref, seg_ref,
                   upool_ref, uconv_ref, usc_ref, umem_ref, q_ref, kt_ref, v_ref):
    xb = _rms(x_ref[...], g_ref[...]).astype(BF16)

    def proj(lo, hi):
        return jnp.dot(xb, w_ref[:, lo:hi], preferred_element_type=F32)

    gw = GROUP_WIDTH
    upool_ref[...] = proj(0, gw)
    uconv_ref[...] = proj(gw, 3 * gw)
    q = proj(3 * gw, 4 * gw)
    k = proj(4 * gw, 5 * gw)
    v_ref[...] = proj(5 * gw, 6 * gw).astype(BF16)
    usc_ref[...] = proj(6 * gw, 9 * gw)
    umem_ref[...] = proj(9 * gw, 10 * gw)
    seg = seg_ref[...]
    q_ref[...] = (_seg_rms(q, gq_ref[...], seg) * (DIFF_HEAD_DIM ** -0.5 * LOG2E)).astype(BF16)
    kt_ref[0] = _seg_rms(k, gk_ref[...], seg).T.astype(BF16)


def _inproj(x2d, g, w, gq, gk, seg32, batch, seq):
    t = x2d.shape[0]
    tm = min(TM_IN, seq)
    nst = seq // tm
    gw = GROUP_WIDTH
    row = lambda c: pl.BlockSpec((tm, c), lambda i: (i, 0))
    full = lambda a: pl.BlockSpec(a.shape, lambda i: (0,) * a.ndim)
    return pl.pallas_call(
        _inproj_kernel,
        grid=(t // tm,),
        in_specs=[row(D_MODEL), full(g), full(w), full(gq), full(gk), full(seg32)],
        out_specs=[row(gw), row(2 * gw), row(3 * gw), row(gw), row(gw),
                   pl.BlockSpec((1, gw, tm), lambda i: (i // nst, 0, i % nst)), row(gw)],
        out_shape=[jax.ShapeDtypeStruct((t, gw), F32), jax.ShapeDtypeStruct((t, 2 * gw), F32),
                   jax.ShapeDtypeStruct((t, 3 * gw), F32), jax.ShapeDtypeStruct((t, gw), F32),
                   jax.ShapeDtypeStruct((t, gw), BF16), jax.ShapeDtypeStruct((batch, gw, seq), BF16),
                   jax.ShapeDtypeStruct((t, gw), BF16)],
        compiler_params=_params(("parallel",)),
        name="in_projection",
    )(x2d, g, w, gq, gk, seg32)


def _memkv_kernel(mem_ref, g_ref, w_ref, gk_ref, seg_ref, kt_ref, v_ref):
    mn = _rms(mem_ref[0], g_ref[...]).astype(BF16)
    kv = jnp.dot(mn, w_ref[...], preferred_element_type=F32)
    k = _seg_rms(kv[:, :GROUP_WIDTH], gk_ref[...], seg_ref[...])
    kt_ref[0] = k.T.astype(BF16)
    v_ref[0] = kv[:, GROUP_WIDTH:].astype(BF16)


def _memkv(mem, g, w, gk, seg64):
    b, m, _ = mem.shape
    gw = GROUP_WIDTH
    full = lambda a: pl.BlockSpec(a.shape, lambda i: (0,) * a.ndim)
    return pl.pallas_call(
        _memkv_kernel,
        grid=(b,),
        in_specs=[pl.BlockSpec((1, m, D_MODEL), lambda i: (i, 0, 0)), full(g), full(w), full(gk), full(seg64)],
        out_specs=[pl.BlockSpec((1, gw, m), lambda i: (i, 0, 0)), pl.BlockSpec((1, m, gw), lambda i: (i, 0, 0))],
        out_shape=[jax.ShapeDtypeStruct((b, gw, m), BF16), jax.ShapeDtypeStruct((b, m, gw), BF16)],
        compiler_params=_params(("parallel",)),
        name="memory_kv",
    )(mem, g, w, gk, seg64)


def _bf16_parts(x, n):
    parts = []
    for _ in range(n):
        bits = np.array([x], np.float32).view(np.uint32)[0]
        bits = (bits + (((bits >> 16) & 1) + 0x7FFF)) & 0xFFFF0000
        p = float(np.array([bits], np.uint32).view(np.float32)[0])
        parts.append(p)
        x -= p
    return parts


def _by_index(idx, values):
    out = jnp.full(idx.shape, values[-1], F32)
    for i in range(len(values) - 2, -1, -1):
        out = jnp.where(idx == i, values[i], out)
    return out


POS_SPLIT = 64
N_LOG2E_PARTS = 4


def _diffattn_kernel(q_ref, kt_ref, v_ref, lam_ref, laminit_ref, gsub_ref, seg_ref, o_ref, s_ref):
    tq = q_ref.shape[0]
    seq = kt_ref.shape[2]
    lp = lam_ref[...]
    lam_init = laminit_ref[...]
    lam = (jnp.exp(jnp.sum(lp[0:1] * lp[1:2], axis=-1, keepdims=True))
           - jnp.exp(jnp.sum(lp[2:3] * lp[3:4], axis=-1, keepdims=True)) + lam_init)
    q0 = pl.multiple_of(pl.program_id(1) * tq, tq)
    q = q_ref[...]
    kt = kt_ref[0]
    v = v_ref[...]
    half = DIFF_HEADS * DIFF_HEAD_DIM
    npart = N_LOG2E_PARTS
    log2e = _bf16_parts(LOG2E, npart)
    arow = lax.broadcasted_iota(jnp.int32, (4 * npart, seq), 0)
    kpos = lax.broadcasted_iota(jnp.int32, (4 * npart, seq), 1)
    sign = jnp.where(kpos < q0, 1.0, jnp.where(kpos >= q0 + tq, -1.0, 0.0))
    k_hi = (kpos & -POS_SPLIT).astype(F32)
    k_lo = (kpos & (POS_SPLIT - 1)).astype(F32)
    k_aug = sign * jnp.where(arow < npart, k_hi, jnp.where(arow < 2 * npart, k_lo, _by_index(arow % npart, log2e)))
    k_aug = jnp.concatenate([k_aug.astype(BF16), jnp.zeros((half - 4 * npart, seq), BF16)], axis=0)
    alane = lax.broadcasted_iota(jnp.int32, (tq, half), 1)
    qpos = q0 + lax.broadcasted_iota(jnp.int32, (tq, half), 0)
    q_hi = (qpos & -POS_SPLIT).astype(F32)
    q_lo = (qpos & (POS_SPLIT - 1)).astype(F32)
    q_aug = jnp.where(alane < 2 * npart, _by_index(alane % npart, log2e),
                      jnp.where(alane < 3 * npart, -q_hi, jnp.where(alane < 4 * npart, -q_lo, 0.0)))
    local = jnp.abs(lax.broadcasted_iota(jnp.int32, (tq, tq), 0)
                    - lax.broadcasted_iota(jnp.int32, (tq, tq), 1)).astype(F32)
    o = jnp.zeros((tq, GROUP_WIDTH), F32)
    for h in range(DIFF_HEADS):
        slope = 2.0 ** (-8.0 * (h + 1) / DIFF_HEADS)
        q_aug_h = (slope * q_aug).astype(BF16)
        own_bias = (slope * LOG2E) * local
        vlo = h * 2 * DIFF_HEAD_DIM
        zlane = (vlo + 2 * DIFF_HEAD_DIM) % GROUP_WIDTH
        v_ones = jnp.where(_lane_range_mask(v.shape, zlane, zlane + 1), jnp.ones_like(v), v)
        outs = []
        for j in range(2):
            qj = q[:, j * half:(j + 1) * half]
            lo = h * DIFF_HEAD_DIM
            qm = jnp.where(_lane_range_mask(qj.shape, lo, lo + DIFF_HEAD_DIM), qj, jnp.zeros_like(qj))
            s_ref[j] = jnp.dot(jnp.concatenate([qm, q_aug_h], axis=1),
                               jnp.concatenate([kt[j * half:(j + 1) * half], k_aug], axis=0),
                               preferred_element_type=F32)
            s_ref[j, :, pl.ds(q0, tq)] = s_ref[j, :, pl.ds(q0, tq)] - own_bias
            s = s_ref[j]
            e = jnp.exp2(s - jnp.max(s, axis=-1, keepdims=True))
            ev = jnp.dot(e.astype(BF16), v_ones, preferred_element_type=F32)
            outs.append(ev * (1.0 / ev[:, zlane:zlane + 1]))
        oh = outs[0] - lam * outs[1]
        o = jnp.where(_lane_range_mask(o.shape, vlo, vlo + 2 * DIFF_HEAD_DIM), oh, o)
    o_ref[...] = _seg_rms(o, gsub_ref[...], seg_ref[...]) * (1.0 - lam_init)


def _diffattn(q, kt, v, lam_p, lam_init, gsub, seg64, batch, seq):
    t = q.shape[0]
    gw = GROUP_WIDTH
    tq = min(TQ_ATT, seq)
    nq = seq // tq
    full = lambda a: pl.BlockSpec(a.shape, lambda b, i: (0,) * a.ndim)
    return pl.pallas_call(
        _diffattn_kernel,
        grid=(batch, nq),
        in_specs=[pl.BlockSpec((tq, gw), lambda b, i: (b * nq + i, 0)),
                  pl.BlockSpec((1, gw, seq), lambda b, i: (b, 0, 0)),
                  pl.BlockSpec((seq, gw), lambda b, i: (b, 0)),
                  full(lam_p), full(lam_init), full(gsub), full(seg64)],
        out_specs=pl.BlockSpec((tq, gw), lambda b, i: (b * nq + i, 0)),
        out_shape=jax.ShapeDtypeStruct((t, gw), F32),
        scratch_shapes=[pltpu.VMEM((2, tq, seq), F32)],
        compiler_params=_params(("parallel", "parallel")),
        name="diff_attention",
    )(q, kt, v, lam_p, lam_init, gsub, seg64)


def _mixer_kernel(x_ref, up_ref, up_prev, up_next, uc_ref, uc_prev, uc_next, us_ref, us_prev, us_next,
                  um_ref, yd_ref, kmt_ref, vm_ref,
                  poolw_ref, pools_ref, dw_ref, lng_ref, lnb_ref, pw_ref, sw_ref, gqm_ref, seg_ref,
                  gn_ref, wout_ref, o_ref, ext_pool, ext_conv, ext_sc, shift_conv, *, seq):
    ts = x_ref.shape[0]
    gw = GROUP_WIDTH
    nst = seq // ts
    it = pl.program_id(0) % nst
    keep_prev = jnp.where(it > 0, 1.0, 0.0)
    keep_next = jnp.where(it < nst - 1, 1.0, 0.0)

    def fill(ext, prev, main, nxt):
        ext[0:HALO, :] = prev * keep_prev
        ext[HALO:HALO + ts, :] = main
        ext[HALO + ts:HALO + ts + HALO, :] = nxt * keep_next

    def shifted(ext, k):
        return ext[HALO + k:HALO + k + ts, :]

    fill(ext_pool, up_prev[...], up_ref[...], up_next[...])
    u = up_ref[...]
    lane = lax.broadcasted_iota(jnp.int32, (ts, gw), 1)
    pos = it * ts + lax.broadcasted_iota(jnp.int32, (ts, gw), 0)
    wsum = jnp.zeros((ts, gw), F32)
    halfw = jnp.zeros((ts, gw), jnp.int32)
    run = None
    prev_half = 0
    for gi, hw in enumerate(POOL_HALF_WINDOWS):
        for k in list(range(-hw, -prev_half)) + list(range(prev_half, hw)):
            term = shifted(ext_pool, k)
            run = term if run is None else run + term
        prev_half = hw
        in_group = (lane >= gi * POOL_CH) & (lane < (gi + 1) * POOL_CH)
        wsum = jnp.where(in_group, run, wsum)
        halfw = jnp.where(in_group, hw, halfw)
    cnt = (jnp.minimum(pos + halfw, seq) - jnp.maximum(pos - halfw, 0)).astype(F32)
    pooled = wsum / cnt - u
    y_pool = jnp.dot(pooled.astype(BF16), poolw_ref[...], preferred_element_type=F32) * pools_ref[...]

    def glu(ucv):
        return ucv[:, :gw] * jax.nn.sigmoid(ucv[:, gw:])

    fill(ext_conv, glu(uc_prev[...]), glu(uc_ref[...]), glu(uc_next[...]))
    pad = CONV_WIDTH // 2
    for r in range(SUBLANES):
        shift_conv[r] = ext_conv[r:r + shift_conv.shape[1], :]
    conv = jnp.zeros((ts, gw), F32)
    for k in range(CONV_WIDTH):
        m, r = divmod(HALO - pad + k, SUBLANES)
        conv = conv + shift_conv[r, SUBLANES * m:SUBLANES * m + ts, :] * dw_ref[k:k + 1, :]
    mu = jnp.mean(conv, axis=-1, keepdims=True)
    cen = conv - mu
    var = jnp.mean(cen * cen, axis=-1, keepdims=True)
    hln = cen * lax.rsqrt(var + EPS) * lng_ref[...] + lnb_ref[...]
    hact = hln * jax.nn.sigmoid(hln)
    y_conv = jnp.dot(hact.astype(BF16), pw_ref[...], preferred_element_type=F32)

    def ch(usv):
        return usv[:, 2 * gw:] * usv[:, :gw]

    fill(ext_sc, ch(us_prev[...]), ch(us_ref[...]), ch(us_next[...]))
    spad = SHORT_CONV_WIDTH // 2
    sconv = jnp.zeros((ts, gw), F32)
    for k in range(SHORT_CONV_WIDTH):
        sconv = sconv + shifted(ext_sc, k - spad) * sw_ref[k:k + 1, :]
    y_sc = us_ref[:, gw:2 * gw] * sconv

    qn = (_seg_rms(um_ref[...], gqm_ref[...], seg_ref[...]) * (MEM_HEAD_DIM ** -0.5)).astype(BF16)
    kmt = kmt_ref[0]
    vm = vm_ref[0]
    y_mem = jnp.zeros((ts, gw), F32)
    for h in range(MEM_HEADS):
        hmask = _lane_range_mask((ts, gw), h * MEM_HEAD_DIM, (h + 1) * MEM_HEAD_DIM)
        qm = jnp.where(hmask, qn, jnp.zeros_like(qn))
        s = jnp.dot(qm, kmt, preferred_element_type=F32)
        e = jnp.exp(s - jnp.max(s, axis=-1, keepdims=True))
        p = (e * (1.0 / jnp.sum(e, axis=-1, keepdims=True))).astype(BF16)
        y_mem = jnp.where(hmask, jnp.dot(p, vm, preferred_element_type=F32), y_mem)

    acc = x_ref[...]
    for gi, y in enumerate((y_pool, y_conv, yd_ref[...], y_sc, y_mem)):
        yn = _rms(y, gn_ref[gi:gi + 1, :]).astype(BF16)
        acc = acc + jnp.dot(yn, wout_ref[gi * gw:(gi + 1) * gw, :], preferred_element_type=F32)
    o_ref[...] = acc


def _mixers(x2d, upool, uconv, usc, umem, ydiff, kmt, vm, poolw, pools, dw, lng, lnb, pw, sw, gqm, seg64,
            gn, wout, batch, seq):
    t = x2d.shape[0]
    gw = GROUP_WIDTH
    ts = min(TS_MIX, seq)
    nst = seq // ts
    r = ts // HALO
    last = t // HALO - 1
    row = lambda c: pl.BlockSpec((ts, c), lambda i: (i, 0))
    prev = lambda c: pl.BlockSpec((HALO, c), lambda i: (jnp.maximum(i * r - 1, 0), 0))
    nxt = lambda c: pl.BlockSpec((HALO, c), lambda i: (jnp.minimum((i + 1) * r, last), 0))
    full = lambda a: pl.BlockSpec(a.shape, lambda i: (0,) * a.ndim)
    perb = lambda a: pl.BlockSpec((1,) + a.shape[1:], lambda i: (i // nst, 0, 0))
    consts = (poolw, pools, dw, lng, lnb, pw, sw, gqm, seg64, gn, wout)
    return pl.pallas_call(
        functools.partial(_mixer_kernel, seq=seq),
        grid=(t // ts,),
        in_specs=[row(D_MODEL),
                  row(gw), prev(gw), nxt(gw),
                  row(2 * gw), prev(2 * gw), nxt(2 * gw),
                  row(3 * gw), prev(3 * gw), nxt(3 * gw),
                  row(gw), row(gw), perb(kmt), perb(vm)] + [full(a) for a in consts],
        out_specs=row(D_MODEL),
        out_shape=jax.ShapeDtypeStruct((t, D_MODEL), F32),
        scratch_shapes=[pltpu.VMEM((ts + 2 * HALO, gw), F32)] * 3
        + [pltpu.VMEM((SUBLANES, ts + 2 * HALO - SUBLANES, gw), F32)],
        compiler_params=_params(("parallel",)),
        name="mixers_out_projection",
    )(x2d, upool, upool, upool, uconv, uconv, uconv, usc, usc, usc, umem, ydiff, kmt, vm, *consts)


def _sorting_network(n):
    pairs = []
    p = 1
    while p < n:
        k = p
        while k >= 1:
            for j in range(k % p, n - k, 2 * k):
                for i in range(min(k, n - j - k)):
                    if (i + j) // (2 * p) == (i + j + k) // (2 * p):
                        pairs.append((i + j, i + j + k))
            k //= 2
        p *= 2
    return pairs


def _sort_blocks_descending(blocks):
    blocks = list(blocks)
    for i, j in _sorting_network(len(blocks)):
        hi, lo = jnp.maximum(blocks[i], blocks[j]), jnp.minimum(blocks[i], blocks[j])
        blocks[i], blocks[j] = hi, lo
    return blocks


def _pop_top_values(stack, singles, n):
    stack, singles = list(stack), list(singles)
    vals = []
    for k in range(n):
        top = stack[0]
        for blk in singles:
            top = jnp.maximum(top, blk)
        m = jnp.max(top, axis=0, keepdims=True)
        vals.append(m)
        keep = min(len(stack), n - k - 1)
        if keep == 0:
            break
        hit = stack[0] == m
        stack = [jnp.where(hit, stack[r + 1] if r + 1 < len(stack) else NEG, stack[r]) for r in range(keep)]
        singles = [jnp.where(blk == m, NEG, blk) for blk in singles]
    return vals


def _retrieval_kernel(x_ref, g_ref, wqt_ref, sk_ref, xnt_ref, thr_ref, e1_ref, e2_ref):
    tm = x_ref.shape[0]
    nk = PEER_N_KEYS
    n = PEER_TOPK + 1
    xb = _rms(x_ref[...], g_ref[...]).T.astype(BF16)
    xnt_ref[...] = xb
    qt = jnp.dot(wqt_ref[...], xb, preferred_element_type=F32)
    sub = SUBLANES
    row_id = lax.broadcasted_iota(jnp.int32, (sub, tm), 0)
    for h in range(PEER_HEADS):
        sc = []
        for j in range(2):
            r0 = (2 * h + j) * nk
            sc.append(jnp.dot(sk_ref[2 * h + j], qt[r0:r0 + nk].astype(BF16), preferred_element_type=F32))
        v1, v2 = (_pop_top_values(_sort_blocks_descending([s[r:r + sub] for r in range(0, nk, sub)]), [], n)
                  for s in sc)
        v2_blocks = []
        for r0 in range(0, n, sub):
            blk = jnp.full((sub, tm), NEG, F32)
            for i in range(r0, min(r0 + sub, n)):
                blk = jnp.where(row_id == i - r0, v2[i], blk)
            v2_blocks.append(blk)
        stack = [v1[0] + v2_blocks[0]]
        for i in range(1, n):
            stack.append(v1[i] + jnp.where(row_id < n // (i + 1), v2_blocks[0], NEG))
        best = _pop_top_values(stack, [v1[0] + blk for blk in v2_blocks[1:]], n)
        z = jnp.zeros((1, tm), F32)
        for i in range(PEER_TOPK):
            z = z + jnp.exp(best[i] - best[0])
        half_inv_z = math.sqrt(0.5) / z
        tau = 0.5 * (best[PEER_TOPK - 1] + best[PEER_TOPK])
        thr = jnp.exp(tau - sc[0] - v2[0]) * half_inv_z
        e1 = jnp.exp(sc[0] - v1[0])
        e2 = jnp.exp(sc[1] - v2[0]) * half_inv_z
        for c in range(tm // LANES):
            cs = slice(c * LANES, (c + 1) * LANES)
            thr_ref[h, c] = thr[:, cs]
            e1_ref[h, c] = e1[:, cs]
            e2_ref[h, c] = e2[:, cs]


def _retrieval(x2d, g, wqt, sk):
    t = x2d.shape[0]
    tm = min(TM_RET, t)
    nk = PEER_N_KEYS
    full = lambda a: pl.BlockSpec(a.shape, lambda i: (0,) * a.ndim)
    tab = pl.BlockSpec((PEER_HEADS, tm // LANES, nk, LANES), lambda i: (0, i, 0, 0))
    tab_shape = jax.ShapeDtypeStruct((PEER_HEADS, t // LANES, nk, LANES), F32)
    return pl.pallas_call(
        _retrieval_kernel,
        grid=(t // tm,),
        in_specs=[pl.BlockSpec((tm, D_MODEL), lambda i: (i, 0)), full(g), full(wqt), full(sk)],
        out_specs=[pl.BlockSpec((D_MODEL, tm), lambda i: (0, i)), tab, tab, tab],
        out_shape=[jax.ShapeDtypeStruct((D_MODEL, t), BF16), tab_shape, tab_shape, tab_shape],
        compiler_params=_params(("parallel",)),
        name="peer_retrieval",
    )(x2d, g, wqt, sk)


def _expert_kernel(x_ref, xnt_ref, u_ref, vt_ref, thr_ref, e1_ref, e2_ref, o_ref, acc_ref, w_ref):
    nk = PEER_N_KEYS
    eb = pl.program_id(1)

    @pl.when(eb == 0)
    def _():
        acc_ref[...] = jnp.zeros_like(acc_ref)

    tm = xnt_ref.shape[1]
    ht = jnp.dot(u_ref[0], xnt_ref[...], preferred_element_type=F32)
    for al in range(A_BLOCK):
        for c in range(tm // LANES):
            cs = slice(c * LANES, (c + 1) * LANES)
            hs = ht[al * nk:(al + 1) * nk, cs]
            gate = None
            for h in range(PEER_HEADS):
                e2 = e2_ref[h, c]
                sel = jnp.where(e2 >= thr_ref[h, c, al:al + 1, :], e2, 0.0).astype(BF16)
                e1 = jnp.broadcast_to(e1_ref[h, c, al:al + 1, :], (2 * SUBLANES, LANES)).astype(BF16)
                term = jnp.tile(e1, (nk // (2 * SUBLANES), 1)) * sel
                gate = term if gate is None else gate + term
            w_ref[al * nk:(al + 1) * nk, cs] = gate * (hs * (1.0 + lax.erf(hs))).astype(BF16)
    acc_ref[...] += jnp.dot(vt_ref[0, 0], w_ref[...], preferred_element_type=F32)

    @pl.when(eb == pl.num_programs(1) - 1)
    def _():
        o_ref[...] = x_ref[...] + acc_ref[...].T


def _experts(x2d, xnt, u, vt, thr, e1, e2, layer):
    t = x2d.shape[0]
    tm = min(TM_EXP, t)
    nk = PEER_N_KEYS
    te = A_BLOCK * nk
    nc = tm // LANES
    tab_a = pl.BlockSpec((PEER_HEADS, nc, A_BLOCK, LANES), lambda i, e: (0, i, e, 0))
    tab_b = pl.BlockSpec((PEER_HEADS, nc, nk, LANES), lambda i, e: (0, i, 0, 0))
    return pl.pallas_call(
        _expert_kernel,
        grid=(t // tm, nk // A_BLOCK),
        in_specs=[pl.BlockSpec((tm, D_MODEL), lambda i, e: (i, 0)),
                  pl.BlockSpec((D_MODEL, tm), lambda i, e: (0, i)),
                  pl.BlockSpec((1, te, D_MODEL), lambda i, e: (layer, e, 0)),
                  pl.BlockSpec((1, 1, D_MODEL, te), lambda i, e: (layer, e, 0, 0)),
                  tab_a, tab_a, tab_b],
        out_specs=pl.BlockSpec((tm, D_MODEL), lambda i, e: (i, 0)),
        out_shape=jax.ShapeDtypeStruct((t, D_MODEL), F32),
        scratch_shapes=[pltpu.VMEM((D_MODEL, tm), F32), pltpu.VMEM((te, tm), BF16)],
        compiler_params=_params(("parallel", "arbitrary")),
        name="peer_experts",
    )(x2d, xnt, u, vt, thr, e1, e2)


def _segment_mean_matrix(width, seg):
    idx = jnp.arange(width) // seg
    return (idx[:, None] == idx[None, :]).astype(F32) / seg


def _block_diag(w):
    g, c, _ = w.shape
    eye = jnp.eye(g, dtype=w.dtype)
    return (eye[:, None, :, None] * w[:, :, None, :]).reshape(g * c, g * c)


def kernel(x, mem, norm_mix, w_in, pool_w, pool_scale, conv_dw, conv_ln_g, conv_ln_b, conv_pw, diff_qk_norm,
           diff_lambda, diff_subln, sconv_w, mem_norm, w_mem_kv, mem_qk_norm, group_norm, w_out, norm_ffn,
           peer_wq, peer_subkeys, peer_u, peer_v):
    b, s, d = x.shape
    depth = w_in.shape[0]
    gw = GROUP_WIDTH
    seg32 = _segment_mean_matrix(gw, DIFF_HEAD_DIM)
    seg64 = _segment_mean_matrix(gw, MEM_HEAD_DIM)
    row = lambda a: a.reshape(1, -1)
    tile = lambda a, n: jnp.tile(a, n).reshape(1, -1)
    x2d = x.reshape(b * s, d)
    u_all = (peer_u * math.sqrt(0.5)).astype(BF16)
    vt_all = peer_v.astype(BF16).reshape(depth, -1, A_BLOCK * PEER_N_KEYS, d).transpose(0, 1, 3, 2)
    for l in range(depth):
        lam_init = jnp.full((1, 1), 0.8 - 0.6 * math.exp(-0.3 * l), F32)
        upool, uconv, usc, umem, q, kt, v = _inproj(
            x2d, row(norm_mix[l]), w_in[l].astype(BF16), tile(diff_qk_norm[l, 0], 2 * DIFF_HEADS),
            tile(diff_qk_norm[l, 1], 2 * DIFF_HEADS), seg32, b, s)
        kmt, vm = _memkv(mem, row(mem_norm[l]), w_mem_kv[l].astype(BF16), tile(mem_qk_norm[l, 1], MEM_HEADS), seg64)
        ydiff = _diffattn(q, kt, v, diff_lambda[l], lam_init, tile(diff_subln[l], DIFF_HEADS), seg64, b, s)
        x2d = _mixers(x2d, upool, uconv, usc, umem, ydiff, kmt, vm,
                      _block_diag(pool_w[l]).astype(BF16), row(pool_scale[l]), conv_dw[l], row(conv_ln_g[l]),
                      row(conv_ln_b[l]), conv_pw[l].astype(BF16), sconv_w[l], tile(mem_qk_norm[l, 0], MEM_HEADS),
                      seg64, group_norm[l].reshape(N_GROUPS, gw), w_out[l].astype(BF16), b, s)
        xnt, thr, e1, e2 = _retrieval(
            x2d, row(norm_ffn[l]), peer_wq[l].T.astype(BF16),
            peer_subkeys[l].reshape(2 * PEER_HEADS, PEER_N_KEYS, -1).astype(BF16))
        x2d = _experts(x2d, xnt, u_all, vt_all, thr, e1, e2, l)
    return x2d.reshape(b, s, d)
```

```python
import functools
import math

import jax
import jax.numpy as jnp
import numpy as np
from jax import lax
from jax.experimental import pallas as pl
from jax.experimental.pallas import tpu as pltpu

F32 = jnp.float32
BF16 = jnp.bfloat16

D_MODEL = 1024
GROUP_WIDTH = 256
N_GROUPS = 5
POOL_HALF_WINDOWS = (1, 2, 4, 8)
POOL_CH = 64
CONV_WIDTH = 31
SHORT_CONV_WIDTH = 3
DIFF_HEADS = 4
DIFF_HEAD_DIM = 32
MEM_HEADS = 4
MEM_HEAD_DIM = 64
PEER_HEADS = 8
PEER_N_KEYS = 128
PEER_TOPK = 16
EPS = 1e-6

LANES = 128
SUBLANES = 8
HALO = 16
VMEM_LIMIT = 56 * 1024 * 1024

TM_IN = 512
TQ_ATT = 256
TS_MIX = 512
TM_RET = 256
TM_EXP = 1024
A_BLOCK = 8
NEG = -1e30
LOG2E = 1.4426950408889634


def _params(sem):
    return pltpu.CompilerParams(dimension_semantics=sem, vmem_limit_bytes=VMEM_LIMIT)


def _rms(x, g):
    return x * lax.rsqrt(jnp.mean(x * x, axis=-1, keepdims=True) + EPS) * g


def _seg_rms(x, g, seg_mean):
    sq = x * x
    hi = sq.astype(BF16)
    lo = (sq - hi.astype(F32)).astype(BF16)
    seg = seg_mean.astype(BF16)
    ms = jnp.dot(hi, seg, preferred_element_type=F32) + jnp.dot(lo, seg, preferred_element_type=F32)
    return x * lax.rsqrt(ms + EPS) * g


def _lane_range_mask(shape, lo, hi):
    lane = lax.broadcasted_iota(jnp.int32, shape, len(shape) - 1)
    return (lane >= lo) & (lane < hi)


def _inproj_kernel(x_ref, g_ref, w_ref, gq_ref, gk_ref, seg_ref,
                   upool_ref, uconv_ref, usc_ref, umem_ref, q_ref, kt_ref, v_ref):
    xb = _rms(x_ref[...], g_ref[...]).astype(BF16)

    def proj(lo, hi):
        return jnp.dot(xb, w_ref[:, lo:hi], preferred_element_type=F32)

    gw = GROUP_WIDTH
    upool_ref[...] = proj(0, gw)
    uconv_ref[...] = proj(gw, 3 * gw)
    q = proj(3 * gw, 4 * gw)
    k = proj(4 * gw, 5 * gw)
    v_ref[...] = proj(5 * gw, 6 * gw).astype(BF16)
    usc_ref[...] = proj(6 * gw, 9 * gw)
    umem_ref[...] = proj(9 * gw, 10 * gw)
    seg = seg_ref[...]
    q_ref[...] = (_seg_rms(q, gq_ref[...], seg) * (DIFF_HEAD_DIM ** -0.5 * LOG2E)).astype(BF16)
    kt_ref[0] = _seg_rms(k, gk_ref[...], seg).T.astype(BF16)


def _inproj(x2d, g, w, gq, gk, seg32, batch, seq):
    t = x2d.shape[0]
    tm = min(TM_IN, seq)
    nst = seq // tm
    gw = GROUP_WIDTH
    row = lambda c: pl.BlockSpec((tm, c), lambda i: (i, 0))
    full = lambda a: pl.BlockSpec(a.shape, lambda i: (0,) * a.ndim)
    return pl.pallas_call(
        _inproj_kernel,
        grid=(t // tm,),
        in_specs=[row(D_MODEL), full(g), full(w), full(gq), full(gk), full(seg32)],
        out_specs=[row(gw), row(2 * gw), row(3 * gw), row(gw), row(gw),
                   pl.BlockSpec((1, gw, tm), lambda i: (i // nst, 0, i % nst)), row(gw)],
        out_shape=[jax.ShapeDtypeStruct((t, gw), F32), jax.ShapeDtypeStruct((t, 2 * gw), F32),
                   jax.ShapeDtypeStruct((t, 3 * gw), F32), jax.ShapeDtypeStruct((t, gw), F32),
                   jax.ShapeDtypeStruct((t, gw), BF16), jax.ShapeDtypeStruct((batch, gw, seq), BF16),
                   jax.ShapeDtypeStruct((t, gw), BF16)],
        compiler_params=_params(("parallel",)),
        name="in_projection",
    )(x2d, g, w, gq, gk, seg32)


def _memkv_kernel(mem_ref, g_ref, w_ref, gk_ref, seg_ref, kt_ref, v_ref):
    mn = _rms(mem_ref[0], g_ref[...]).astype(BF16)
    kv = jnp.dot(mn, w_ref[...], preferred_element_type=F32)
    k = _seg_rms(kv[:, :GROUP_WIDTH], gk_ref[...], seg_ref[...])
    kt_ref[0] = k.T.astype(BF16)
    v_ref[0] = kv[:, GROUP_WIDTH:].astype(BF16)


def _memkv(mem, g, w, gk, seg64):
    b, m, _ = mem.shape
    gw = GROUP_WIDTH
    full = lambda a: pl.BlockSpec(a.shape, lambda i: (0,) * a.ndim)
    return pl.pallas_call(
        _memkv_kernel,
        grid=(b,),
        in_specs=[pl.BlockSpec((1, m, D_MODEL), lambda i: (i, 0, 0)), full(g), full(w), full(gk), full(seg64)],
        out_specs=[pl.BlockSpec((1, gw, m), lambda i: (i, 0, 0)), pl.BlockSpec((1, m, gw), lambda i: (i, 0, 0))],
        out_shape=[jax.ShapeDtypeStruct((b, gw, m), BF16), jax.ShapeDtypeStruct((b, m, gw), BF16)],
        compiler_params=_params(("parallel",)),
        name="memory_kv",
    )(mem, g, w, gk, seg64)


def _bf16_parts(x, n):
    parts = []
    for _ in range(n):
        bits = np.array([x], np.float32).view(np.uint32)[0]
        bits = (bits + (((bits >> 16) & 1) + 0x7FFF)) & 0xFFFF0000
        p = float(np.array([bits], np.uint32).view(np.float32)[0])
        parts.append(p)
        x -= p
    return parts


def _by_index(idx, values):
    out = jnp.full(idx.shape, values[-1], F32)
    for i in range(len(values) - 2, -1, -1):
        out = jnp.where(idx == i, values[i], out)
    return out


POS_SPLIT = 64
N_LOG2E_PARTS = 4


def _diffattn_kernel(q_ref, kt_ref, v_ref, lam_ref, laminit_ref, gsub_ref, seg_ref, o_ref, s_ref):
    tq = q_ref.shape[0]
    seq = kt_ref.shape[2]
    lp = lam_ref[...]
    lam_init = laminit_ref[...]
    lam = (jnp.exp(jnp.sum(lp[0:1] * lp[1:2], axis=-1, keepdims=True))
           - jnp.exp(jnp.sum(lp[2:3] * lp[3:4], axis=-1, keepdims=True)) + lam_init)
    q0 = pl.multiple_of(pl.program_id(1) * tq, tq)
    q = q_ref[...]
    kt = kt_ref[0]
    v = v_ref[...]
    half = DIFF_HEADS * DIFF_HEAD_DIM
    npart = N_LOG2E_PARTS
    log2e = _bf16_parts(LOG2E, npart)
    arow = lax.broadcasted_iota(jnp.int32, (4 * npart, seq), 0)
    kpos = lax.broadcasted_iota(jnp.int32, (4 * npart, seq), 1)
    sign = jnp.where(kpos < q0, 1.0, jnp.where(kpos >= q0 + tq, -1.0, 0.0))
    k_hi = (kpos & -POS_SPLIT).astype(F32)
    k_lo = (kpos & (POS_SPLIT - 1)).astype(F32)
    k_aug = sign * jnp.where(arow < npart, k_hi, jnp.where(arow < 2 * npart, k_lo, _by_index(arow % npart, log2e)))
    k_aug = jnp.concatenate([k_aug.astype(BF16), jnp.zeros((half - 4 * npart, seq), BF16)], axis=0)
    alane = lax.broadcasted_iota(jnp.int32, (tq, half), 1)
    qpos = q0 + lax.broadcasted_iota(jnp.int32, (tq, half), 0)
    q_hi = (qpos & -POS_SPLIT).astype(F32)
    q_lo = (qpos & (POS_SPLIT - 1)).astype(F32)
    q_aug = jnp.where(alane < 2 * npart, _by_index(alane % npart, log2e),
                      jnp.where(alane < 3 * npart, -q_hi, jnp.where(alane < 4 * npart, -q_lo, 0.0)))
    local = jnp.abs(lax.broadcasted_iota(jnp.int32, (tq, tq), 0)
                    - lax.broadcasted_iota(jnp.int32, (tq, tq), 1)).astype(F32)
    o = jnp.zeros((tq, GROUP_WIDTH), F32)
    for h in range(DIFF_HEADS):
        slope = 2.0 ** (-8.0 * (h + 1) / DIFF_HEADS)
        q_aug_h = (slope * q_aug).astype(BF16)
        own_bias = (slope * LOG2E) * local
        vlo = h * 2 * DIFF_HEAD_DIM
        zlane = (vlo + 2 * DIFF_HEAD_DIM) % GROUP_WIDTH
        v_ones = jnp.where(_lane_range_mask(v.shape, zlane, zlane + 1), jnp.ones_like(v), v)
        outs = []
        for j in range(2):
            qj = q[:, j * half:(j + 1) * half]
            lo = h * DIFF_HEAD_DIM
            qm = jnp.where(_lane_range_mask(qj.shape, lo, lo + DIFF_HEAD_DIM), qj, jnp.zeros_like(qj))
            s_ref[j] = jnp.dot(jnp.concatenate([qm, q_aug_h], axis=1),
                               jnp.concatenate([kt[j * half:(j + 1) * half], k_aug], axis=0),
                               preferred_element_type=F32)
            s_ref[j, :, pl.ds(q0, tq)] = s_ref[j, :, pl.ds(q0, tq)] - own_bias
            s = s_ref[j]
            e = jnp.exp2(s - jnp.max(s, axis=-1, keepdims=True))
            ev = jnp.dot(e.astype(BF16), v_ones, preferred_element_type=F32)
            outs.append(ev * (1.0 / ev[:, zlane:zlane + 1]))
        oh = outs[0] - lam * outs[1]
        o = jnp.where(_lane_range_mask(o.shape, vlo, vlo + 2 * DIFF_HEAD_DIM), oh, o)
    o_ref[...] = _seg_rms(o, gsub_ref[...], seg_ref[...]) * (1.0 - lam_init)


def _diffattn(q, kt, v, lam_p, lam_init, gsub, seg64, batch, seq):
    t = q.shape[0]
    gw = GROUP_WIDTH
    tq = min(TQ_ATT, seq)
    nq = seq // tq
    full = lambda a: pl.BlockSpec(a.shape, lambda b, i: (0,) * a.ndim)
    return pl.pallas_call(
        _diffattn_kernel,
        grid=(batch, nq),
        in_specs=[pl.BlockSpec((tq, gw), lambda b, i: (b * nq + i, 0)),
                  pl.BlockSpec((1, gw, seq), lambda b, i: (b, 0, 0)),
                  pl.BlockSpec((seq, gw), lambda b, i: (b, 0)),
                  full(lam_p), full(lam_init), full(gsub), full(seg64)],
        out_specs=pl.BlockSpec((tq, gw), lambda b, i: (b * nq + i, 0)),
        out_shape=jax.ShapeDtypeStruct((t, gw), F32),
        scratch_shapes=[pltpu.VMEM((2, tq, seq), F32)],
        compiler_params=_params(("parallel", "parallel")),
        name="diff_attention",
    )(q, kt, v, lam_p, lam_init, gsub, seg64)


def _mixer_kernel(x_ref, up_ref, up_prev, up_next, uc_ref, uc_prev, uc_next, us_ref, us_prev, us_next,
                  um_ref, yd_ref, kmt_ref, vm_ref,
                  poolw_ref, pools_ref, dw_ref, lng_ref, lnb_ref, pw_ref, sw_ref, gqm_ref, seg_ref,
                  gn_ref, wout_ref, o_ref, ext_pool, ext_conv, ext_sc, shift_conv, *, seq):
    ts = x_ref.shape[0]
    gw = GROUP_WIDTH
    nst = seq // ts
    it = pl.program_id(0) % nst
    keep_prev = jnp.where(it > 0, 1.0, 0.0)
    keep_next = jnp.where(it < nst - 1, 1.0, 0.0)

    def fill(ext, prev, main, nxt):
        ext[0:HALO, :] = prev * keep_prev
        ext[HALO:HALO + ts, :] = main
        ext[HALO + ts:HALO + ts + HALO, :] = nxt * keep_next

    def shifted(ext, k):
        return ext[HALO + k:HALO + k + ts, :]

    fill(ext_pool, up_prev[...], up_ref[...], up_next[...])
    u = up_ref[...]
    lane = lax.broadcasted_iota(jnp.int32, (ts, gw), 1)
    pos = it * ts + lax.broadcasted_iota(jnp.int32, (ts, gw), 0)
    wsum = jnp.zeros((ts, gw), F32)
    halfw = jnp.zeros((ts, gw), jnp.int32)
    run = None
    prev_half = 0
    for gi, hw in enumerate(POOL_HALF_WINDOWS):
        for k in list(range(-hw, -prev_half)) + list(range(prev_half, hw)):
            term = shifted(ext_pool, k)
            run = term if run is None else run + term
        prev_half = hw
        in_group = (lane >= gi * POOL_CH) & (lane < (gi + 1) * POOL_CH)
        wsum = jnp.where(in_group, run, wsum)
        halfw = jnp.where(in_group, hw, halfw)
    cnt = (jnp.minimum(pos + halfw, seq) - jnp.maximum(pos - halfw, 0)).astype(F32)
    pooled = wsum / cnt - u
    y_pool = jnp.dot(pooled.astype(BF16), poolw_ref[...], preferred_element_type=F32) * pools_ref[...]

    def glu(ucv):
        return ucv[:, :gw] * jax.nn.sigmoid(ucv[:, gw:])

    fill(ext_conv, glu(uc_prev[...]), glu(uc_ref[...]), glu(uc_next[...]))
    pad = CONV_WIDTH // 2
    for r in range(SUBLANES):
        shift_conv[r] = ext_conv[r:r + shift_conv.shape[1], :]
    conv = jnp.zeros((ts, gw), F32)
    for k in range(CONV_WIDTH):
        m, r = divmod(HALO - pad + k, SUBLANES)
        conv = conv + shift_conv[r, SUBLANES * m:SUBLANES * m + ts, :] * dw_ref[k:k + 1, :]
    mu = jnp.mean(conv, axis=-1, keepdims=True)
    cen = conv - mu
    var = jnp.mean(cen * cen, axis=-1, keepdims=True)
    hln = cen * lax.rsqrt(var + EPS) * lng_ref[...] + lnb_ref[...]
    hact = hln * jax.nn.sigmoid(hln)
    y_conv = jnp.dot(hact.astype(BF16), pw_ref[...], preferred_element_type=F32)

    def ch(usv):
        return usv[:, 2 * gw:] * usv[:, :gw]

    fill(ext_sc, ch(us_prev[...]), ch(us_ref[...]), ch(us_next[...]))
    spad = SHORT_CONV_WIDTH // 2
    sconv = jnp.zeros((ts, gw), F32)
    for k in range(SHORT_CONV_WIDTH):
        sconv = sconv + shifted(ext_sc, k - spad) * sw_ref[k:k + 1, :]
    y_sc = us_ref[:, gw:2 * gw] * sconv

    qn = (_seg_rms(um_ref[...], gqm_ref[...], seg_ref[...]) * (MEM_HEAD_DIM ** -0.5)).astype(BF16)
    kmt = kmt_ref[0]
    vm = vm_ref[0]
    y_mem = jnp.zeros((ts, gw), F32)
    for h in range(MEM_HEADS):
        hmask = _lane_range_mask((ts, gw), h * MEM_HEAD_DIM, (h + 1) * MEM_HEAD_DIM)
        qm = jnp.where(hmask, qn, jnp.zeros_like(qn))
        s = jnp.dot(qm, kmt, preferred_element_type=F32)
        e = jnp.exp(s - jnp.max(s, axis=-1, keepdims=True))
        p = (e * (1.0 / jnp.sum(e, axis=-1, keepdims=True))).astype(BF16)
        y_mem = jnp.where(hmask, jnp.dot(p, vm, preferred_element_type=F32), y_mem)

    acc = x_ref[...]
    for gi, y in enumerate((y_pool, y_conv, yd_ref[...], y_sc, y_mem)):
        yn = _rms(y, gn_ref[gi:gi + 1, :]).astype(BF16)
        acc = acc + jnp.dot(yn, wout_ref[gi * gw:(gi + 1) * gw, :], preferred_element_type=F32)
    o_ref[...] = acc


def _mixers(x2d, upool, uconv, usc, umem, ydiff, kmt, vm, poolw, pools, dw, lng, lnb, pw, sw, gqm, seg64,
            gn, wout, batch, seq):
    t = x2d.shape[0]
    gw = GROUP_WIDTH
    ts = min(TS_MIX, seq)
    nst = seq // ts
    r = ts // HALO
    last = t // HALO - 1
    row = lambda c: pl.BlockSpec((ts, c), lambda i: (i, 0))
    prev = lambda c: pl.BlockSpec((HALO, c), lambda i: (jnp.maximum(i * r - 1, 0), 0))
    nxt = lambda c: pl.BlockSpec((HALO, c), lambda i: (jnp.minimum((i + 1) * r, last), 0))
    full = lambda a: pl.BlockSpec(a.shape, lambda i: (0,) * a.ndim)
    perb = lambda a: pl.BlockSpec((1,) + a.shape[1:], lambda i: (i // nst, 0, 0))
    consts = (poolw, pools, dw, lng, lnb, pw, sw, gqm, seg64, gn, wout)
    return pl.pallas_call(
        functools.partial(_mixer_kernel, seq=seq),
        grid=(t // ts,),
        in_specs=[row(D_MODEL),
                  row(gw), prev(gw), nxt(gw),
                  row(2 * gw), prev(2 * gw), nxt(2 * gw),
                  row(3 * gw), prev(3 * gw), nxt(3 * gw),
                  row(gw), row(gw), perb(kmt), perb(vm)] + [full(a) for a in consts],
        out_specs=row(D_MODEL),
        out_shape=jax.ShapeDtypeStruct((t, D_MODEL), F32),
        scratch_shapes=[pltpu.VMEM((ts + 2 * HALO, gw), F32)] * 3
        + [pltpu.VMEM((SUBLANES, ts + 2 * HALO - SUBLANES, gw), F32)],
        compiler_params=_params(("parallel",)),
        name="mixers_out_projection",
    )(x2d, upool, upool, upool, uconv, uconv, uconv, usc, usc, usc, umem, ydiff, kmt, vm, *consts)


def _sorting_network(n):
    pairs = []
    p = 1
    while p < n:
        k = p
        while k >= 1:
            for j in range(k % p, n - k, 2 * k):
                for i in range(min(k, n - j - k)):
                    if (i + j) // (2 * p) == (i + j + k) // (2 * p):
                        pairs.append((i + j, i + j + k))
            k //= 2
        p *= 2
    return pairs


def _sort_blocks_descending(blocks):
    blocks = list(blocks)
    for i, j in _sorting_network(len(blocks)):
        hi, lo = jnp.maximum(blocks[i], blocks[j]), jnp.minimum(blocks[i], blocks[j])
        blocks[i], blocks[j] = hi, lo
    return blocks


def _pop_top_values(stack, singles, n):
    stack, singles = list(stack), list(singles)
    vals = []
    for k in range(n):
        top = stack[0]
        for blk in singles:
            top = jnp.maximum(top, blk)
        m = jnp.max(top, axis=0, keepdims=True)
        vals.append(m)
        keep = min(len(stack), n - k - 1)
        if keep == 0:
            break
        hit = stack[0] == m
        stack = [jnp.where(hit, stack[r + 1] if r + 1 < len(stack) else NEG, stack[r]) for r in range(keep)]
        singles = [jnp.where(blk == m, NEG, blk) for blk in singles]
    return vals


def _retrieval_kernel(x_ref, g_ref, wqt_ref, sk_ref, xnt_ref, thr_ref, e1_ref, e2_ref):
    tm = x_ref.shape[0]
    nk = PEER_N_KEYS
    n = PEER_TOPK + 1
    xb = _rms(x_ref[...], g_ref[...]).T.astype(BF16)
    xnt_ref[...] = xb
    qt = jnp.dot(wqt_ref[...], xb, preferred_element_type=F32)
    sub = SUBLANES
    row_id = lax.broadcasted_iota(jnp.int32, (sub, tm), 0)
    for h in range(PEER_HEADS):
        sc = []
        for j in range(2):
            r0 = (2 * h + j) * nk
            sc.append(jnp.dot(sk_ref[2 * h + j], qt[r0:r0 + nk].astype(BF16), preferred_element_type=F32))
        v1, v2 = (_pop_top_values(_sort_blocks_descending([s[r:r + sub] for r in range(0, nk, sub)]), [], n)
                  for s in sc)
        v2_blocks = []
        for r0 in range(0, n, sub):
            blk = jnp.full((sub, tm), NEG, F32)
            for i in range(r0, min(r0 + sub, n)):
                blk = jnp.where(row_id == i - r0, v2[i], blk)
            v2_blocks.append(blk)
        stack = [v1[0] + v2_blocks[0]]
        for i in range(1, n):
            stack.append(v1[i] + jnp.where(row_id < n // (i + 1), v2_blocks[0], NEG))
        best = _pop_top_values(stack, [v1[0] + blk for blk in v2_blocks[1:]], n)
        z = jnp.zeros((1, tm), F32)
        for i in range(PEER_TOPK):
            z = z + jnp.exp(best[i] - best[0])
        half_inv_z = math.sqrt(0.5) / z
        tau = 0.5 * (best[PEER_TOPK - 1] + best[PEER_TOPK])
        thr = jnp.exp(tau - sc[0] - v2[0]) * half_inv_z
        e1 = jnp.exp(sc[0] - v1[0])
        e2 = jnp.exp(sc[1] - v2[0]) * half_inv_z
        for c in range(tm // LANES):
            cs = slice(c * LANES, (c + 1) * LANES)
            thr_ref[h, c] = thr[:, cs]
            e1_ref[h, c] = e1[:, cs]
            e2_ref[h, c] = e2[:, cs]


def _retrieval(x2d, g, wqt, sk):
    t = x2d.shape[0]
    tm = min(TM_RET, t)
    nk = PEER_N_KEYS
    full = lambda a: pl.BlockSpec(a.shape, lambda i: (0,) * a.ndim)
    tab = pl.BlockSpec((PEER_HEADS, tm // LANES, nk, LANES), lambda i: (0, i, 0, 0))
    tab_shape = jax.ShapeDtypeStruct((PEER_HEADS, t // LANES, nk, LANES), F32)
    return pl.pallas_call(
        _retrieval_kernel,
        grid=(t // tm,),
        in_specs=[pl.BlockSpec((tm, D_MODEL), lambda i: (i, 0)), full(g), full(wqt), full(sk)],
        out_specs=[pl.BlockSpec((D_MODEL, tm), lambda i: (0, i)), tab, tab, tab],
        out_shape=[jax.ShapeDtypeStruct((D_MODEL, t), BF16), tab_shape, tab_shape, tab_shape],
        compiler_params=_params(("parallel",)),
        name="peer_retrieval",
    )(x2d, g, wqt, sk)


def _expert_kernel(x_ref, xnt_ref, u_ref, vt_ref, thr_ref, e1_ref, e2_ref, o_ref, acc_ref, w_ref):
    nk = PEER_N_KEYS
    eb = pl.program_id(1)

    @pl.when(eb == 0)
    def _():
        acc_ref[...] = jnp.zeros_like(acc_ref)

    tm = xnt_ref.shape[1]
    ht = jnp.dot(u_ref[0], xnt_ref[...], preferred_element_type=F32)
    for al in range(A_BLOCK):
        for c in range(tm // LANES):
            cs = slice(c * LANES, (c + 1) * LANES)
            hs = ht[al * nk:(al + 1) * nk, cs]
            gate = None
            for h in range(PEER_HEADS):
                e2 = e2_ref[h, c]
                sel = jnp.where(e2 >= thr_ref[h, c, al:al + 1, :], e2, 0.0).astype(BF16)
                e1 = jnp.broadcast_to(e1_ref[h, c, al:al + 1, :], (2 * SUBLANES, LANES)).astype(BF16)
                term = jnp.tile(e1, (nk // (2 * SUBLANES), 1)) * sel
                gate = term if gate is None else gate + term
            w_ref[al * nk:(al + 1) * nk, cs] = gate * (hs * (1.0 + lax.erf(hs))).astype(BF16)
    acc_ref[...] += jnp.dot(vt_ref[0, 0], w_ref[...], preferred_element_type=F32)

    @pl.when(eb == pl.num_programs(1) - 1)
    def _():
        o_ref[...] = x_ref[...] + acc_ref[...].T


def _experts(x2d, xnt, u, vt, thr, e1, e2, layer):
    t = x2d.shape[0]
    tm = min(TM_EXP, t)
    nk = PEER_N_KEYS
    te = A_BLOCK * nk
    nc = tm // LANES
    tab_a = pl.BlockSpec((PEER_HEADS, nc, A_BLOCK, LANES), lambda i, e: (0, i, e, 0))
    tab_b = pl.BlockSpec((PEER_HEADS, nc, nk, LANES), lambda i, e: (0, i, 0, 0))
    return pl.pallas_call(
        _expert_kernel,
        grid=(t // tm, nk // A_BLOCK),
        in_specs=[pl.BlockSpec((tm, D_MODEL), lambda i, e: (i, 0)),
                  pl.BlockSpec((D_MODEL, tm), lambda i, e: (0, i)),
                  pl.BlockSpec((1, te, D_MODEL), lambda i, e: (layer, e, 0)),
                  pl.BlockSpec((1, 1, D_MODEL, te), lambda i, e: (layer, e, 0, 0)),
                  tab_a, tab_a, tab_b],
        out_specs=pl.BlockSpec((tm, D_MODEL), lambda i, e: (i, 0)),
        out_shape=jax.ShapeDtypeStruct((t, D_MODEL), F32),
        scratch_shapes=[pltpu.VMEM((D_MODEL, tm), F32), pltpu.VMEM((te, tm), BF16)],
        compiler_params=_params(("parallel", "arbitrary")),
        name="peer_experts",
    )(x2d, xnt, u, vt, thr, e1, e2)


def _segment_mean_matrix(width, seg):
    idx = jnp.arange(width) // seg
    return (idx[:, None] == idx[None, :]).astype(F32) / seg


def _block_diag(w):
    g, c, _ = w.shape
    eye = jnp.eye(g, dtype=w.dtype)
    return (eye[:, None, :, None] * w[:, :, None, :]).reshape(g * c, g * c)


def kernel(x, mem, norm_mix, w_in, pool_w, pool_scale, conv_dw, conv_ln_g, conv_ln_b, conv_pw, diff_qk_norm,
           diff_lambda, diff_subln, sconv_w, mem_norm, w_mem_kv, mem_qk_norm, group_norm, w_out, norm_ffn,
           peer_wq, peer_subkeys, peer_u, peer_v):
    b, s, d = x.shape
    depth = w_in.shape[0]
    gw = GROUP_WIDTH
    seg32 = _segment_mean_matrix(gw, DIFF_HEAD_DIM)
    seg64 = _segment_mean_matrix(gw, MEM_HEAD_DIM)
    row = lambda a: a.reshape(1, -1)
    tile = lambda a, n: jnp.tile(a, n).reshape(1, -1)
    x2d = x.reshape(b * s, d)
    u_all = (peer_u * math.sqrt(0.5)).astype(BF16)
    vt_all = peer_v.astype(BF16).reshape(depth, -1, A_BLOCK * PEER_N_KEYS, d).transpose(0, 1, 3, 2)
    for l in range(depth):
        lam_init = jnp.full((1, 1), 0.8 - 0.6 * math.exp(-0.3 * l), F32)
        upool, uconv, usc, umem, q, kt, v = _inproj(
            x2d, row(norm_mix[l]), w_in[l].astype(BF16), tile(diff_qk_norm[l, 0], 2 * DIFF_HEADS),
            tile(diff_qk_norm[l, 1], 2 * DIFF_HEADS), seg32, b, s)
        kmt, vm = _memkv(mem, row(mem_norm[l]), w_mem_kv[l].astype(BF16), tile(mem_qk_norm[l, 1], MEM_HEADS), seg64)
        ydiff = _diffattn(q, kt, v, diff_lambda[l], lam_init, tile(diff_subln[l], DIFF_HEADS), seg64, b, s)
        x2d = _mixers(x2d, upool, uconv, usc, umem, ydiff, kmt, vm,
                      _block_diag(pool_w[l]).astype(BF16), row(pool_scale[l]), conv_dw[l], row(conv_ln_g[l]),
                      row(conv_ln_b[l]), conv_pw[l].astype(BF16), sconv_w[l], tile(mem_qk_norm[l, 0], MEM_HEADS),
                      seg64, group_norm[l].reshape(N_GROUPS, gw), w_out[l].astype(BF16), b, s)
        xnt, thr, e1, e2 = _retrieval(
            x2d, row(norm_ffn[l]), peer_wq[l].T.astype(BF16),
            peer_subkeys[l].reshape(2 * PEER_HEADS, PEER_N_KEYS, -1).astype(BF16))
        x2d = _experts(x2d, xnt, u_all, vt_all, thr, e1, e2, l)
    return x2d.reshape(b, s, d)
```
